```python
import jax, jax.numpy as jnp
from jax import lax
import numpy as np

D_MODEL = 1024
BATCH = 8
SEQ = 4096
DEPTH = 4

PLE_DIM = 256
GROUP_WIDTH = D_MODEL // 8
FNET_GROUPS = 4
SGU_GROUPS = 4
CONF_GROUPS = 4
SCONV_GROUPS = 4
FNET_WIDTH = FNET_GROUPS * GROUP_WIDTH
SGU_WIDTH = SGU_GROUPS * GROUP_WIDTH
CONF_WIDTH = CONF_GROUPS * GROUP_WIDTH
SCONV_WIDTH = SCONV_GROUPS * GROUP_WIDTH
EVEN_IN = FNET_WIDTH + 2 * SGU_WIDTH
ODD_IN = 2 * CONF_WIDTH + 3 * SCONV_WIDTH
MIX_WIDTH = FNET_WIDTH + SGU_WIDTH
CHUNK = 128
CONF_KERNEL = 31
SCONV_KERNEL = 3
FFN_KERNEL = 3
D_FF = ((8 * D_MODEL // 3 + 127) // 128) * 128
EPS = 1e-6
N_EVEN = (DEPTH + 1) // 2
N_ODD = DEPTH // 2

kernel_name = "hybrid_fnet_sgu_conformer_shortconv_encoder"


def rmsnorm(x, g):
    xf = x.astype(jnp.float32)
    y = xf * lax.rsqrt(jnp.mean(xf * xf, axis=-1, keepdims=True) + EPS)
    return (y * g.astype(jnp.float32)).astype(x.dtype)


def layernorm(x, g, b):
    xf = x.astype(jnp.float32)
    mu = jnp.mean(xf, axis=-1, keepdims=True)
    xc = xf - mu
    var = jnp.mean(xc * xc, axis=-1, keepdims=True)
    y = xc * lax.rsqrt(var + EPS) * g.astype(jnp.float32) + b.astype(jnp.float32)
    return y.astype(x.dtype)


def depthwise_conv(x, w, b=None):
    k = w.shape[0]
    c = x.shape[-1]
    y = lax.conv_general_dilated(
        x, w[:, None, :].astype(x.dtype), window_strides=(1,),
        padding=[(k // 2, k // 2)], dimension_numbers=('NWC', 'WIO', 'NWC'),
        feature_group_count=c)
    if b is not None:
        y = y + b.astype(x.dtype)
    return y


def fourier_mixer(a, w_f):
    bn, s, _ = a.shape
    a4 = a.reshape(bn, s, FNET_GROUPS, GROUP_WIDTH).astype(jnp.float32)
    f = jnp.fft.fft2(a4, axes=(1, 3), norm="ortho").real
    y = jnp.einsum('bsgc,gcd->bsgd', f, w_f.astype(jnp.float32))
    return y.reshape(bn, s, FNET_WIDTH).astype(a.dtype)


def spatial_gating(u, v, ln_g, ln_b, w_s, b_s):
    bn, s, _ = v.shape
    v5 = v.reshape(bn, s // CHUNK, CHUNK, SGU_GROUPS, GROUP_WIDTH)
    v5 = layernorm(v5, ln_g.reshape(SGU_GROUPS, GROUP_WIDTH), ln_b.reshape(SGU_GROUPS, GROUP_WIDTH))
    sv = jnp.einsum('gpq,bnqgc->bnpgc', w_s.astype(v5.dtype), v5)
    sv = sv + b_s.T.astype(v5.dtype)[None, None, :, :, None]
    return u * sv.reshape(bn, s, SGU_WIDTH)


def conformer_conv(z, conv_w, conv_b, ln_g, ln_b):
    bn, s, _ = z.shape
    a, gate = jnp.split(z, 2, axis=-1)
    g = a * jax.nn.sigmoid(gate)
    c = depthwise_conv(g, conv_w, conv_b)
    c = layernorm(c.reshape(bn, s, CONF_GROUPS, GROUP_WIDTH),
                  ln_g.reshape(CONF_GROUPS, GROUP_WIDTH), ln_b.reshape(CONF_GROUPS, GROUP_WIDTH))
    return jax.nn.silu(c.reshape(bn, s, CONF_WIDTH))


def short_gated_conv(bg, cg, xin, w):
    return bg * depthwise_conv(cg * xin, w)


def setup_inputs(seed: int = 0) -> dict:
    key = jax.random.key(seed)
    ks = jax.random.split(key, 32)

    def nrm(k, shape, scale):
        return jax.random.normal(k, shape, jnp.float32) * scale

    def gain(k, shape):
        return 1.0 + 0.05 * jax.random.normal(k, shape, jnp.float32)

    return {
        "x": nrm(ks[0], (BATCH, SEQ, D_MODEL), 1.0),
        "p": nrm(ks[1], (DEPTH, BATCH, SEQ, PLE_DIM), 1.0),
        "mix_pre_g": gain(ks[2], (DEPTH, D_MODEL)),
        "mix_post_g": gain(ks[3], (DEPTH, D_MODEL)),
        "ffn_pre_g": gain(ks[4], (DEPTH, D_MODEL)),
        "ffn_post_g": gain(ks[5], (DEPTH, D_MODEL)),
        "ev_w_in": nrm(ks[6], (N_EVEN, D_MODEL, EVEN_IN), D_MODEL ** -0.5),
        "ev_w_fourier": nrm(ks[7], (N_EVEN, FNET_GROUPS, GROUP_WIDTH, GROUP_WIDTH), GROUP_WIDTH ** -0.5),
        "ev_v_ln_g": gain(ks[8], (N_EVEN, SGU_WIDTH)),
        "ev_v_ln_b": nrm(ks[9], (N_EVEN, SGU_WIDTH), 0.02),
        "ev_w_spatial": nrm(ks[10], (N_EVEN, SGU_GROUPS, CHUNK, CHUNK), CHUNK ** -0.5),
        "ev_b_spatial": 1.0 + nrm(ks[11], (N_EVEN, SGU_GROUPS, CHUNK), 0.1),
        "ev_w_out": nrm(ks[12], (N_EVEN, MIX_WIDTH, D_MODEL), MIX_WIDTH ** -0.5),
        "od_w_in": nrm(ks[13], (N_ODD, D_MODEL, ODD_IN), D_MODEL ** -0.5),
        "od_conv_w": nrm(ks[14], (N_ODD, CONF_KERNEL, CONF_WIDTH), CONF_KERNEL ** -0.5),
        "od_conv_b": nrm(ks[15], (N_ODD, CONF_WIDTH), 0.02),
        "od_ln_g": gain(ks[16], (N_ODD, CONF_WIDTH)),
        "od_ln_b": nrm(ks[17], (N_ODD, CONF_WIDTH), 0.02),
        "od_sconv_w": nrm(ks[18], (N_ODD, SCONV_KERNEL, SCONV_WIDTH), SCONV_KERNEL ** -0.5),
        "od_w_out": nrm(ks[19], (N_ODD, MIX_WIDTH, D_MODEL), MIX_WIDTH ** -0.5),
        "ffn_w_up": nrm(ks[20], (DEPTH, D_MODEL, 2 * D_FF), D_MODEL ** -0.5),
        "ffn_conv_w": nrm(ks[21], (DEPTH, FFN_KERNEL, 2 * D_FF), FFN_KERNEL ** -0.5),
        "ffn_conv_b": nrm(ks[22], (DEPTH, 2 * D_FF), 0.02),
        "ffn_w_down": nrm(ks[23], (DEPTH, D_FF, D_MODEL), D_FF ** -0.5),
        "ple_w_p": nrm(ks[24], (DEPTH, PLE_DIM, D_MODEL), PLE_DIM ** -0.5),
        "ple_gate_g": gain(ks[25], (DEPTH, D_MODEL)),
        "ple_w_g": nrm(ks[26], (DEPTH, D_MODEL, D_MODEL), D_MODEL ** -0.5),
        "ple_b_g": nrm(ks[27], (DEPTH, D_MODEL), 0.02),
    }


def reference(x, p, mix_pre_g, mix_post_g, ffn_pre_g, ffn_post_g,
              ev_w_in, ev_w_fourier, ev_v_ln_g, ev_v_ln_b, ev_w_spatial, ev_b_spatial, ev_w_out,
              od_w_in, od_conv_w, od_conv_b, od_ln_g, od_ln_b, od_sconv_w, od_w_out,
              ffn_w_up, ffn_conv_w, ffn_conv_b, ffn_w_down,
              ple_w_p, ple_gate_g, ple_w_g, ple_b_g):
    for i in range(DEPTH):
        j = i // 2
        h = rmsnorm(x, mix_pre_g[i])
        if i % 2 == 0:
            z = h @ ev_w_in[j]
            za = z[..., :FNET_WIDTH]
            zuv = jax.nn.gelu(z[..., FNET_WIDTH:])
            zu, zv = jnp.split(zuv, 2, axis=-1)
            ya = fourier_mixer(za, ev_w_fourier[j])
            yb = spatial_gating(zu, zv, ev_v_ln_g[j], ev_v_ln_b[j], ev_w_spatial[j], ev_b_spatial[j])
            y = jnp.concatenate([ya, yb], axis=-1) @ ev_w_out[j]
        else:
            z = h @ od_w_in[j]
            zc = z[..., :2 * CONF_WIDTH]
            bg, cg, xin = jnp.split(z[..., 2 * CONF_WIDTH:], 3, axis=-1)
            yc = conformer_conv(zc, od_conv_w[j], od_conv_b[j], od_ln_g[j], od_ln_b[j])
            yd = short_gated_conv(bg, cg, xin, od_sconv_w[j])
            y = jnp.concatenate([yc, yd], axis=-1) @ od_w_out[j]
        x = x + rmsnorm(y, mix_post_g[i])
        h = rmsnorm(x, ffn_pre_g[i])
        up = depthwise_conv(h @ ffn_w_up[i], ffn_conv_w[i], ffn_conv_b[i])
        g, u = jnp.split(up, 2, axis=-1)
        f = (jax.nn.gelu(g) * u) @ ffn_w_down[i]
        x = x + rmsnorm(f, ffn_post_g[i])
        gate = jax.nn.sigmoid(rmsnorm(x, ple_gate_g[i]) @ ple_w_g[i] + ple_b_g[i])
        x = x + gate * (p[i] @ ple_w_p[i])
    return x
```

```python
import functools
import math

import jax
import jax.numpy as jnp
import numpy as np
from jax import lax
from jax.experimental import pallas as pl
from jax.experimental.pallas import tpu as pltpu

EPS = 1e-6
LANES = 128
GROUPS = 4
BRANCH = GROUPS * LANES
CHUNK = 128
CONF_K = 31
CONF_HALO = 16
FFN_HALO = 8
V7X_VMEM_LIMIT = 56 * 1024 * 1024

F32 = jnp.float32
BF16 = jnp.bfloat16


def _rms(x, g):
    ms = jnp.mean(x * x, axis=-1, keepdims=True)
    return x * lax.rsqrt(ms + EPS) * g


def _group_layernorm(x, g, b):
    outs = []
    for k in range(GROUPS):
        sl = slice(k * LANES, (k + 1) * LANES)
        xs = x[:, sl]
        mu = jnp.mean(xs, axis=-1, keepdims=True)
        xc = xs - mu
        var = jnp.mean(xc * xc, axis=-1, keepdims=True)
        outs.append(xc * lax.rsqrt(var + EPS) * g[:, sl] + b[:, sl])
    return jnp.concatenate(outs, axis=-1)


def _dot(a, b):
    return jnp.dot(a, b, preferred_element_type=F32)


def _resident(shape):
    nd = len(shape)
    return pl.BlockSpec(shape, lambda *_: (0,) * nd, pipeline_mode=pl.Buffered(1))


def _params(n_grid_axes):
    return pltpu.CompilerParams(
        dimension_semantics=("arbitrary",) * n_grid_axes,
        vmem_limit_bytes=V7X_VMEM_LIMIT,
    )


def _even_in_kernel(x_ref, g_ref, win_ref, wf_ref, lng_ref, lnb_ref, ws_ref, bs_ref,
                    z_ref, yb_ref, *, ts):
    h = _rms(x_ref[0], g_ref[...]).astype(BF16)
    z = _dot(h, win_ref[...])
    za = z[:, :BRANCH].astype(BF16)
    for k in range(GROUPS):
        sl = slice(k * LANES, (k + 1) * LANES)
        t = _dot(za[:, sl], wf_ref[k])
        z_ref[0, :, sl] = t[:, :LANES].astype(BF16)
        z_ref[0, :, BRANCH + k * LANES:BRANCH + (k + 1) * LANES] = t[:, LANES:].astype(BF16)
    zuv = jax.nn.gelu(z[:, BRANCH:])
    zu = zuv[:, :BRANCH]
    vn = _group_layernorm(zuv[:, BRANCH:], lng_ref[...], lnb_ref[...]).astype(BF16)
    for k in range(GROUPS):
        sl = slice(k * LANES, (k + 1) * LANES)
        wsk = ws_ref[k]
        bsk = bs_ref[k]
        for c in range(ts // CHUNK):
            rows = slice(c * CHUNK, (c + 1) * CHUNK)
            sv = _dot(wsk, vn[rows, sl]) + bsk
            yb_ref[0, rows, sl] = (zu[rows, sl] * sv).astype(BF16)


def _even_in(x, g, w_in, wf, ln_g, ln_b, ws, bs, ts):
    b, s, d = x.shape
    return pl.pallas_call(
        functools.partial(_even_in_kernel, ts=ts),
        grid=(b, s // ts),
        in_specs=[
            pl.BlockSpec((1, ts, d), lambda i, j: (i, j, 0)),
            _resident(g.shape), _resident(w_in.shape), _resident(wf.shape),
            _resident(ln_g.shape), _resident(ln_b.shape), _resident(ws.shape), _resident(bs.shape),
        ],
        out_specs=[
            pl.BlockSpec((1, ts, 2 * BRANCH), lambda i, j: (i, j, 0)),
            pl.BlockSpec((1, ts, BRANCH), lambda i, j: (i, j, 0)),
        ],
        out_shape=[
            jax.ShapeDtypeStruct((b, s, 2 * BRANCH), BF16),
            jax.ShapeDtypeStruct((b, s, BRANCH), BF16),
        ],
        compiler_params=_params(2),
        name="even_in",
    )(x, g, w_in, wf, ln_g, ln_b, ws, bs)


def _seq_dft_kernel(cs_ref, z_ref, o_ref, *, s):
    acc = _dot(cs_ref[:, :s], z_ref[0, :, :BRANCH]) + _dot(cs_ref[:, s:], z_ref[0, :, BRANCH:])
    o_ref[0] = acc.astype(BF16)


def _seq_dft(cs, z, tk):
    b, s, _ = z.shape
    return pl.pallas_call(
        functools.partial(_seq_dft_kernel, s=s),
        grid=(s // tk, b),
        in_specs=[
            pl.BlockSpec((tk, 2 * s), lambda k, i: (k, 0)),
            pl.BlockSpec((1, s, 2 * BRANCH), lambda k, i: (i, 0, 0)),
        ],
        out_specs=pl.BlockSpec((1, tk, BRANCH), lambda k, i: (i, k, 0)),
        out_shape=jax.ShapeDtypeStruct((b, s, BRANCH), BF16),
        compiler_params=_params(2),
        name="seq_dft",
    )(cs, z)


def _mix_out_kernel(x_ref, ya_ref, yb_ref, wout_ref, g_ref, o_ref):
    y = _dot(ya_ref[0], wout_ref[:BRANCH, :]) + _dot(yb_ref[0], wout_ref[BRANCH:, :])
    o_ref[0] = x_ref[0] + _rms(y, g_ref[...])


def _mix_out(x, ya, yb, w_out, g, ts):
    b, s, d = x.shape
    return pl.pallas_call(
        _mix_out_kernel,
        grid=(b, s // ts),
        in_specs=[
            pl.BlockSpec((1, ts, d), lambda i, j: (i, j, 0)),
            pl.BlockSpec((1, ts, BRANCH), lambda i, j: (i, j, 0)),
            pl.BlockSpec((1, ts, BRANCH), lambda i, j: (i, j, 0)),
            _resident(w_out.shape), _resident(g.shape),
        ],
        out_specs=pl.BlockSpec((1, ts, d), lambda i, j: (i, j, 0)),
        out_shape=jax.ShapeDtypeStruct(x.shape, F32),
        compiler_params=_params(2),
        name="mix_out",
    )(x, ya, yb, w_out, g)


def _halo_specs(ts, halo, d, s):
    per = ts // halo
    last = s // halo - 1
    return [
        pl.BlockSpec((1, ts, d), lambda i, j: (i, j, 0)),
        pl.BlockSpec((1, halo, d), lambda i, j: (i, jnp.maximum(j * per - 1, 0), 0)),
        pl.BlockSpec((1, halo, d), lambda i, j: (i, jnp.minimum((j + 1) * per, last), 0)),
    ]


def _normed_rows(x_ref, prev_ref, next_ref, g):
    j = pl.program_id(1)
    h_main = _rms(x_ref[0], g).astype(BF16)
    h_prev = jnp.where(j > 0, _rms(prev_ref[0], g), 0.0)
    h_next = jnp.where(j < pl.num_programs(1) - 1, _rms(next_ref[0], g), 0.0)
    h_halo = jnp.concatenate([h_prev, h_next], axis=0).astype(BF16)
    return jnp.concatenate([h_main, h_halo], axis=0)


def _stage_rows(ext_ref, v, ts, halo):
    ext_ref[0:halo, :] = v[ts:ts + halo, :]
    ext_ref[halo:halo + ts, :] = v[0:ts, :]
    ext_ref[halo + ts:2 * halo + ts, :] = v[ts + halo:ts + 2 * halo, :]


def _conv3(ext_ref, w, ts, halo):
    return (ext_ref[halo - 1:halo - 1 + ts, :] * w[0:1, :]
            + ext_ref[halo:halo + ts, :] * w[1:2, :]
            + ext_ref[halo + 1:halo + 1 + ts, :] * w[2:3, :])


CONV_ROWS = 64


def _odd_mix_kernel(x_ref, prev_ref, next_ref, pre_g_ref, win_ref, cw_ref, cb_ref, lng_ref, lnb_ref,
                    sw_ref, wout_ref, post_g_ref, o_ref, gext_ref, mext_ref, c_ref, *, ts):
    halo = CONF_HALO
    h = _normed_rows(x_ref, prev_ref, next_ref, pre_g_ref[...])
    z = _dot(h, win_ref[...])
    glu = z[:, :BRANCH] * jax.nn.sigmoid(z[:, BRANCH:2 * BRANCH])
    _stage_rows(gext_ref, glu, ts, halo)
    _stage_rows(mext_ref, z[:, 3 * BRANCH:4 * BRANCH] * z[:, 4 * BRANCH:5 * BRANCH], ts, halo)
    bg = z[:ts, 2 * BRANCH:3 * BRANCH]

    off = halo - CONF_K // 2
    for k in range(GROUPS):
        sl = slice(k * LANES, (k + 1) * LANES)
        for r in range(ts // CONV_ROWS):
            r0 = r * CONV_ROWS
            acc = gext_ref[r0 + off:r0 + off + CONV_ROWS, sl] * cw_ref[0:1, sl]
            for t in range(1, CONF_K):
                acc = acc + gext_ref[r0 + off + t:r0 + off + t + CONV_ROWS, sl] * cw_ref[t:t + 1, sl]
            c_ref[r0:r0 + CONV_ROWS, sl] = acc
    c = _group_layernorm(c_ref[...] + cb_ref[...], lng_ref[...], lnb_ref[...])
    yc = jax.nn.silu(c).astype(BF16)
    yd = (bg * _conv3(mext_ref, sw_ref[...], ts, halo)).astype(BF16)
    y = _dot(yc, wout_ref[:BRANCH, :]) + _dot(yd, wout_ref[BRANCH:, :])
    o_ref[0] = x_ref[0] + _rms(y, post_g_ref[...])


def _odd_mix(x, pre_g, w_in, conv_w, conv_b, ln_g, ln_b, sconv_w, w_out, post_g, ts):
    b, s, d = x.shape
    halo = CONF_HALO
    consts = (pre_g, w_in, conv_w, conv_b, ln_g, ln_b, sconv_w, w_out, post_g)
    return pl.pallas_call(
        functools.partial(_odd_mix_kernel, ts=ts),
        grid=(b, s // ts),
        in_specs=_halo_specs(ts, halo, d, s) + [_resident(c.shape) for c in consts],
        out_specs=pl.BlockSpec((1, ts, d), lambda i, j: (i, j, 0)),
        out_shape=jax.ShapeDtypeStruct(x.shape, F32),
        scratch_shapes=[
            pltpu.VMEM((ts + 2 * halo, BRANCH), F32),
            pltpu.VMEM((ts + 2 * halo, BRANCH), F32),
            pltpu.VMEM((ts, BRANCH), F32),
        ],
        compiler_params=_params(2),
        name="odd_mix",
    )(x, x, x, *consts)


FF_CHUNK = 256


def _ffn_ple_kernel(x_ref, prev_ref, next_ref, p_ref, pre_g_ref, wup_ref, cw_ref, cb_ref, wdown_ref,
                    post_g_ref, gate_g_ref, wg_ref, bg_ref, wp_ref, o_ref, gext_ref, uext_ref, *, ts, d_ff):
    halo = FFN_HALO
    h = _normed_rows(x_ref, prev_ref, next_ref, pre_g_ref[...])
    f = jnp.zeros((ts, x_ref.shape[-1]), F32)
    for c in range(d_ff // FF_CHUNK):
        gs = slice(c * FF_CHUNK, (c + 1) * FF_CHUNK)
        us = slice(d_ff + c * FF_CHUNK, d_ff + (c + 1) * FF_CHUNK)
        _stage_rows(gext_ref, _dot(h, wup_ref[:, gs]), ts, halo)
        _stage_rows(uext_ref, _dot(h, wup_ref[:, us]), ts, halo)
        gate = _conv3(gext_ref, cw_ref[:, gs], ts, halo) + cb_ref[:, gs]
        up = _conv3(uext_ref, cw_ref[:, us], ts, halo) + cb_ref[:, us]
        act = (jax.nn.gelu(gate) * up).astype(BF16)
        f = f + _dot(act, wdown_ref[gs, :])
    x2 = x_ref[0] + _rms(f, post_g_ref[...])
    hg = _rms(x2, gate_g_ref[...]).astype(BF16)
    gate = jax.nn.sigmoid(_dot(hg, wg_ref[...]) + bg_ref[...])
    o_ref[0] = x2 + gate * _dot(p_ref[0, 0].astype(BF16), wp_ref[...])


def _ffn_ple(x, p, layer, pre_g, w_up, conv_w, conv_b, w_down, post_g, gate_g, w_g, b_g, w_p, ts):
    b, s, d = x.shape
    halo = FFN_HALO
    d_ff = w_down.shape[0]
    consts = (pre_g, w_up, conv_w, conv_b, w_down, post_g, gate_g, w_g, b_g, w_p)
    return pl.pallas_call(
        functools.partial(_ffn_ple_kernel, ts=ts, d_ff=d_ff),
        grid=(b, s // ts),
        in_specs=_halo_specs(ts, halo, d, s)
        + [pl.BlockSpec((1, 1, ts, p.shape[-1]), lambda i, j: (layer, i, j, 0))]
        + [_resident(c.shape) for c in consts],
        out_specs=pl.BlockSpec((1, ts, d), lambda i, j: (i, j, 0)),
        out_shape=jax.ShapeDtypeStruct(x.shape, F32),
        scratch_shapes=[
            pltpu.VMEM((ts + 2 * halo, FF_CHUNK), F32),
            pltpu.VMEM((ts + 2 * halo, FF_CHUNK), F32),
        ],
        compiler_params=_params(2),
        name="ffn_ple",
    )(x, x, x, p, *consts)


def _channel_dft_weights(w_f, s):
    n = np.arange(LANES)
    ang = 2.0 * np.pi * ((n[:, None] * n[None, :]) % LANES) / LANES
    scale = 1.0 / math.sqrt(float(s) * LANES)
    cos = jnp.asarray(np.cos(ang) * scale, F32)
    sin = jnp.asarray(-np.sin(ang) * scale, F32)
    re = jnp.einsum("cd,gde->gce", cos, w_f, precision=lax.Precision.HIGHEST)
    im = jnp.einsum("cd,gde->gce", sin, w_f, precision=lax.Precision.HIGHEST)
    return jnp.concatenate([re, im], axis=-1).astype(BF16)


def _seq_dft_table(s):
    k = lax.broadcasted_iota(jnp.int32, (s, s), 0)
    n = lax.broadcasted_iota(jnp.int32, (s, s), 1)
    ang = ((k * n) % s).astype(F32) * (2.0 * math.pi / s)
    return jnp.concatenate([jnp.cos(ang), jnp.sin(ang)], axis=1).astype(BF16)


def _row(v):
    return v.reshape(1, -1)


def kernel(x, p, mix_pre_g, mix_post_g, ffn_pre_g, ffn_post_g, ev_w_in, ev_w_fourier, ev_v_ln_g, ev_v_ln_b, ev_w_spatial, ev_b_spatial, ev_w_out, od_w_in, od_conv_w, od_conv_b, od_ln_g, od_ln_b, od_sconv_w, od_w_out, ffn_w_up, ffn_conv_w, ffn_conv_b, ffn_w_down, ple_w_p, ple_gate_g, ple_w_g, ple_b_g):
    depth = mix_pre_g.shape[0]
    s = x.shape[1]
    ts = min(512, s)
    cs = _seq_dft_table(s)
    for i in range(depth):
        j = i // 2
        if i % 2 == 0:
            wf = _channel_dft_weights(ev_w_fourier[j], s)
            bs = jnp.broadcast_to(ev_b_spatial[j][:, :, None], (GROUPS, CHUNK, LANES))
            z, yb = _even_in(x, _row(mix_pre_g[i]), ev_w_in[j].astype(BF16), wf,
                             _row(ev_v_ln_g[j]), _row(ev_v_ln_b[j]),
                             ev_w_spatial[j].astype(BF16), bs, ts)
            ya = _seq_dft(cs, z, ts)
            x = _mix_out(x, ya, yb, ev_w_out[j].astype(BF16), _row(mix_post_g[i]), ts)
        else:
            x = _odd_mix(x, _row(mix_pre_g[i]), od_w_in[j].astype(BF16), od_conv_w[j], _row(od_conv_b[j]),
                         _row(od_ln_g[j]), _row(od_ln_b[j]), od_sconv_w[j], od_w_out[j].astype(BF16),
                         _row(mix_post_g[i]), ts)
        x = _ffn_ple(x, p, i, _row(ffn_pre_g[i]), ffn_w_up[i].astype(BF16), ffn_conv_w[i],
                     _row(ffn_conv_b[i]), ffn_w_down[i].astype(BF16), _row(ffn_post_g[i]),
                     _row(ple_gate_g[i]), ple_w_g[i].astype(BF16), _row(ple_b_g[i]),
                     ple_w_p[i].astype(BF16), ts)
    return x
```

```python
import functools
import math

import jax
import jax.numpy as jnp
import numpy as np
from jax import lax
from jax.experimental import pallas as pl
from jax.experimental.pallas import tpu as pltpu

EPS = 1e-6
LANES = 128
GROUPS = 4
BRANCH = GROUPS * LANES
CHUNK = 128
CONF_K = 31
CONF_HALO = 16
FFN_HALO = 8
V7X_VMEM_LIMIT = 56 * 1024 * 1024

F32 = jnp.float32
BF16 = jnp.bfloat16


def _rms(x, g):
    ms = jnp.mean(x * x, axis=-1, keepdims=True)
    return x * lax.rsqrt(ms + EPS) * g


def _group_layernorm(x, g, b):
    outs = []
    for k in range(GROUPS):
        sl = slice(k * LANES, (k + 1) * LANES)
        xs = x[:, sl]
        mu = jnp.mean(xs, axis=-1, keepdims=True)
        xc = xs - mu
        var = jnp.mean(xc * xc, axis=-1, keepdims=True)
        outs.append(xc * lax.rsqrt(var + EPS) * g[:, sl] + b[:, sl])
    return jnp.concatenate(outs, axis=-1)


def _dot(a, b):
    return jnp.dot(a, b, preferred_element_type=F32)


def _resident(shape):
    nd = len(shape)
    return pl.BlockSpec(shape, lambda *_: (0,) * nd, pipeline_mode=pl.Buffered(1))


def _params(n_grid_axes):
    return pltpu.CompilerParams(
        dimension_semantics=("arbitrary",) * n_grid_axes,
        vmem_limit_bytes=V7X_VMEM_LIMIT,
    )


def _even_in_kernel(x_ref, g_ref, win_ref, wf_ref, lng_ref, lnb_ref, ws_ref, bs_ref,
                    z_ref, yb_ref, *, ts):
    h = _rms(x_ref[0], g_ref[...]).astype(BF16)
    z = _dot(h, win_ref[...])
    za = z[:, :BRANCH].astype(BF16)
    for k in range(GROUPS):
        sl = slice(k * LANES, (k + 1) * LANES)
        t = _dot(za[:, sl], wf_ref[k])
        z_ref[0, :, sl] = t[:, :LANES].astype(BF16)
        z_ref[0, :, BRANCH + k * LANES:BRANCH + (k + 1) * LANES] = t[:, LANES:].astype(BF16)
    zuv = jax.nn.gelu(z[:, BRANCH:])
    zu = zuv[:, :BRANCH]
    vn = _group_layernorm(zuv[:, BRANCH:], lng_ref[...], lnb_ref[...]).astype(BF16)
    for k in range(GROUPS):
        sl = slice(k * LANES, (k + 1) * LANES)
        wsk = ws_ref[k]
        bsk = bs_ref[k]
        for c in range(ts // CHUNK):
            rows = slice(c * CHUNK, (c + 1) * CHUNK)
            sv = _dot(wsk, vn[rows, sl]) + bsk
            yb_ref[0, rows, sl] = (zu[rows, sl] * sv).astype(BF16)


def _even_in(x, g, w_in, wf, ln_g, ln_b, ws, bs, ts):
    b, s, d = x.shape
    return pl.pallas_call(
        functools.partial(_even_in_kernel, ts=ts),
        grid=(b, s // ts),
        in_specs=[
            pl.BlockSpec((1, ts, d), lambda i, j: (i, j, 0)),
            _resident(g.shape), _resident(w_in.shape), _resident(wf.shape),
            _resident(ln_g.shape), _resident(ln_b.shape), _resident(ws.shape), _resident(bs.shape),
        ],
        out_specs=[
            pl.BlockSpec((1, ts, 2 * BRANCH), lambda i, j: (i, j, 0)),
            pl.BlockSpec((1, ts, BRANCH), lambda i, j: (i, j, 0)),
        ],
        out_shape=[
            jax.ShapeDtypeStruct((b, s, 2 * BRANCH), BF16),
            jax.ShapeDtypeStruct((b, s, BRANCH), BF16),
        ],
        compiler_params=_params(2),
        name="even_in",
    )(x, g, w_in, wf, ln_g, ln_b, ws, bs)


def _seq_dft_kernel(cs_ref, z_ref, o_ref, *, s):
    acc = _dot(cs_ref[:, :s], z_ref[0, :, :BRANCH]) + _dot(cs_ref[:, s:], z_ref[0, :, BRANCH:])
    o_ref[0] = acc.astype(BF16)


def _seq_dft(cs, z, tk):
    b, s, _ = z.shape
    return pl.pallas_call(
        functools.partial(_seq_dft_kernel, s=s),
        grid=(s // tk, b),
        in_specs=[
            pl.BlockSpec((tk, 2 * s), lambda k, i: (k, 0)),
            pl.BlockSpec((1, s, 2 * BRANCH), lambda k, i: (i, 0, 0)),
        ],
        out_specs=pl.BlockSpec((1, tk, BRANCH), lambda k, i: (i, k, 0)),
        out_shape=jax.ShapeDtypeStruct((b, s, BRANCH), BF16),
        compiler_params=_params(2),
        name="seq_dft",
    )(cs, z)


def _mix_out_kernel(x_ref, ya_ref, yb_ref, wout_ref, g_ref, o_ref):
    y = _dot(ya_ref[0], wout_ref[:BRANCH, :]) + _dot(yb_ref[0], wout_ref[BRANCH:, :])
    o_ref[0] = x_ref[0] + _rms(y, g_ref[...])


def _mix_out(x, ya, yb, w_out, g, ts):
    b, s, d = x.shape
    return pl.pallas_call(
        _mix_out_kernel,
        grid=(b, s // ts),
        in_specs=[
            pl.BlockSpec((1, ts, d), lambda i, j: (i, j, 0)),
            pl.BlockSpec((1, ts, BRANCH), lambda i, j: (i, j, 0)),
            pl.BlockSpec((1, ts, BRANCH), lambda i, j: (i, j, 0)),
            _resident(w_out.shape), _resident(g.shape),
        ],
        out_specs=pl.BlockSpec((1, ts, d), lambda i, j: (i, j, 0)),
        out_shape=jax.ShapeDtypeStruct(x.shape, F32),
        compiler_params=_params(2),
        name="mix_out",
    )(x, ya, yb, w_out, g)


def _halo_specs(ts, halo, d, s):
    per = ts // halo
    last = s // halo - 1
    return [
        pl.BlockSpec((1, ts, d), lambda i, j: (i, j, 0)),
        pl.BlockSpec((1, halo, d), lambda i, j: (i, jnp.maximum(j * per - 1, 0), 0)),
        pl.BlockSpec((1, halo, d), lambda i, j: (i, jnp.minimum((j + 1) * per, last), 0)),
    ]


def _normed_rows(x_ref, prev_ref, next_ref, g):
    j = pl.program_id(1)
    h_main = _rms(x_ref[0], g).astype(BF16)
    h_prev = jnp.where(j > 0, _rms(prev_ref[0], g), 0.0)
    h_next = jnp.where(j < pl.num_programs(1) - 1, _rms(next_ref[0], g), 0.0)
    h_halo = jnp.concatenate([h_prev, h_next], axis=0).astype(BF16)
    return jnp.concatenate([h_main, h_halo], axis=0)


ROW_PITCH = 2


def _stage_slab(slab_ref, v, ts, halo):
    slab_ref[pl.ds(0, halo, stride=ROW_PITCH), :] = v[ts:ts + halo, :]
    slab_ref[pl.ds(ROW_PITCH * halo, ts, stride=ROW_PITCH), :] = v[0:ts, :]
    slab_ref[pl.ds(ROW_PITCH * (halo + ts), halo, stride=ROW_PITCH), :] = v[ts + halo:ts + 2 * halo, :]


def _slab_rows(slab_ref, first, n):
    return slab_ref[pl.ds(ROW_PITCH * first, n, stride=ROW_PITCH), :]


CONV_ROWS = 64


def _odd_mix_kernel(x_ref, prev_ref, next_ref, pre_g_ref, win_ref, cw_ref, cb_ref, lng_ref, lnb_ref,
                    sw_ref, wout_ref, post_g_ref, o_ref, gslab_ref, mslab_ref, *, ts):
    halo = CONF_HALO
    pair = 2 * LANES
    h = _normed_rows(x_ref, prev_ref, next_ref, pre_g_ref[...])

    def proj(first_col, width):
        return _dot(h, win_ref[:, first_col:first_col + width])

    def glu_conv_norm(a, gate, first_group):
        glu = a * jax.nn.sigmoid(gate)
        outs = []
        for l in range(2):
            k = first_group + l
            sl = slice(k * LANES, (k + 1) * LANES)
            slab = gslab_ref.at[k]
            _stage_slab(slab, glu[:, l * LANES:(l + 1) * LANES], ts, halo)
            first = halo - CONF_K // 2
            blocks = []
            for r in range(ts // CONV_ROWS):
                r0 = r * CONV_ROWS + first
                acc = _slab_rows(slab, r0, CONV_ROWS) * cw_ref[0:1, sl]
                for t in range(1, CONF_K):
                    acc = acc + _slab_rows(slab, r0 + t, CONV_ROWS) * cw_ref[t:t + 1, sl]
                blocks.append(acc)
            c = jnp.concatenate(blocks, axis=0) + cb_ref[:, sl]
            mu = jnp.mean(c, axis=-1, keepdims=True)
            cc = c - mu
            var = jnp.mean(cc * cc, axis=-1, keepdims=True)
            outs.append(jax.nn.silu(cc * lax.rsqrt(var + EPS) * lng_ref[:, sl] + lnb_ref[:, sl]))
        return jnp.concatenate(outs, axis=-1).astype(BF16)

    a0, g0 = proj(0, pair), proj(BRANCH, pair)
    a1, g1 = proj(pair, pair), proj(BRANCH + pair, pair)
    yc0 = glu_conv_norm(a0, g0, 0)
    bg = proj(2 * BRANCH, BRANCH)[:ts]
    cg = proj(3 * BRANCH, BRANCH)
    xin = proj(4 * BRANCH, BRANCH)
    yc1 = glu_conv_norm(a1, g1, 2)
    y = _dot(yc0, wout_ref[0:pair, :]) + _dot(yc1, wout_ref[pair:BRANCH, :])
    m = cg * xin
    yd = []
    for k in range(GROUPS):
        sl = slice(k * LANES, (k + 1) * LANES)
        slab = mslab_ref.at[k]
        _stage_slab(slab, m[:, sl], ts, halo)
        conv = (_slab_rows(slab, halo - 1, ts) * sw_ref[0:1, sl] + m[0:ts, sl] * sw_ref[1:2, sl]
                + _slab_rows(slab, halo + 1, ts) * sw_ref[2:3, sl])
        yd.append(bg[:, sl] * conv)
    y = y + _dot(jnp.concatenate(yd, axis=-1).astype(BF16), wout_ref[BRANCH:, :])
    o_ref[0] = x_ref[0] + _rms(y, post_g_ref[...])


def _odd_mix(x, pre_g, w_in, conv_w, conv_b, ln_g, ln_b, sconv_w, w_out, post_g, ts):
    b, s, d = x.shape
    halo = CONF_HALO
    consts = (pre_g, w_in, conv_w, conv_b, ln_g, ln_b, sconv_w, w_out, post_g)
    return pl.pallas_call(
        functools.partial(_odd_mix_kernel, ts=ts),
        grid=(b, s // ts),
        in_specs=_halo_specs(ts, halo, d, s) + [_resident(c.shape) for c in consts],
        out_specs=pl.BlockSpec((1, ts, d), lambda i, j: (i, j, 0)),
        out_shape=jax.ShapeDtypeStruct(x.shape, F32),
        scratch_shapes=[
            pltpu.VMEM((GROUPS, ROW_PITCH * (ts + 2 * halo), LANES), F32),
            pltpu.VMEM((GROUPS, ROW_PITCH * (ts + 2 * halo), LANES), F32),
        ],
        compiler_params=_params(2),
        name="odd_mix",
    )(x, x, x, *consts)


FF_CHUNK = 256


def _ffn_ple_kernel(x_ref, prev_ref, next_ref, p_ref, pre_g_ref, wup_ref, cw_ref, cb_ref, wdown_ref,
                    post_g_ref, gate_g_ref, wg_ref, bg_ref, wp_ref, o_ref, ext_ref, *, ts, d_ff):
    halo = FFN_HALO
    n_slab = FF_CHUNK // LANES
    h = _normed_rows(x_ref, prev_ref, next_ref, pre_g_ref[...])
    f = jnp.zeros((ts, x_ref.shape[-1]), F32)
    n_chunks = d_ff // FF_CHUNK

    def up_proj(c):
        return [_dot(h, wup_ref[:, base + c * FF_CHUNK:base + (c + 1) * FF_CHUNK]) for base in (0, d_ff)]

    z_next = up_proj(0)
    for c in range(n_chunks):
        z_cur = z_next
        if c + 1 < n_chunks:
            z_next = up_proj(c + 1)
        halves = []
        for half, base in enumerate((0, d_ff)):
            cols = slice(base + c * FF_CHUNK, base + (c + 1) * FF_CHUNK)
            zc = z_cur[half]
            w = cw_ref[:, cols]
            bias = cb_ref[:, cols]
            outs = []
            for l in range(n_slab):
                lanes = slice(l * LANES, (l + 1) * LANES)
                slab = ext_ref.at[c % 2, half * n_slab + l]
                _stage_slab(slab, zc[:, lanes], ts, halo)
                outs.append(_slab_rows(slab, halo - 1, ts) * w[0:1, lanes]
                            + zc[0:ts, lanes] * w[1:2, lanes]
                            + _slab_rows(slab, halo + 1, ts) * w[2:3, lanes]
                            + bias[:, lanes])
            halves.append(jnp.concatenate(outs, axis=-1))
        act = (jax.nn.gelu(halves[0]) * halves[1]).astype(BF16)
        f = f + _dot(act, wdown_ref[c * FF_CHUNK:(c + 1) * FF_CHUNK, :])
    x2 = x_ref[0] + _rms(f, post_g_ref[...])
    hg = _rms(x2, gate_g_ref[...]).astype(BF16)
    gate = jax.nn.sigmoid(_dot(hg, wg_ref[...]) + bg_ref[...])
    o_ref[0] = x2 + gate * _dot(p_ref[0, 0].astype(BF16), wp_ref[...])


def _ffn_ple(x, p, layer, pre_g, w_up, conv_w, conv_b, w_down, post_g, gate_g, w_g, b_g, w_p, ts):
    b, s, d = x.shape
    halo = FFN_HALO
    d_ff = w_down.shape[0]
    consts = (pre_g, w_up, conv_w, conv_b, w_down, post_g, gate_g, w_g, b_g, w_p)
    return pl.pallas_call(
        functools.partial(_ffn_ple_kernel, ts=ts, d_ff=d_ff),
        grid=(b, s // ts),
        in_specs=_halo_specs(ts, halo, d, s)
        + [pl.BlockSpec((1, 1, ts, p.shape[-1]), lambda i, j: (layer, i, j, 0))]
        + [_resident(c.shape) for c in consts],
        out_specs=pl.BlockSpec((1, ts, d), lambda i, j: (i, j, 0)),
        out_shape=jax.ShapeDtypeStruct(x.shape, F32),
        scratch_shapes=[
            pltpu.VMEM((2, 2 * FF_CHUNK // LANES, ROW_PITCH * (ts + 2 * halo), LANES), F32),
        ],
        compiler_params=_params(2),
        name="ffn_ple",
    )(x, x, x, p, *consts)


def _channel_dft_weights(w_f, s):
    n = np.arange(LANES)
    ang = 2.0 * np.pi * ((n[:, None] * n[None, :]) % LANES) / LANES
    scale = 1.0 / math.sqrt(float(s) * LANES)
    cos = jnp.asarray(np.cos(ang) * scale, F32)
    sin = jnp.asarray(-np.sin(ang) * scale, F32)
    re = jnp.einsum("cd,gde->gce", cos, w_f, precision=lax.Precision.HIGHEST)
    im = jnp.einsum("cd,gde->gce", sin, w_f, precision=lax.Precision.HIGHEST)
    return jnp.concatenate([re, im], axis=-1).astype(BF16)


def _seq_dft_table(s):
    k = lax.broadcasted_iota(jnp.int32, (s, s), 0)
    n = lax.broadcasted_iota(jnp.int32, (s, s), 1)
    ang = ((k * n) % s).astype(F32) * (2.0 * math.pi / s)
    return jnp.concatenate([jnp.cos(ang), jnp.sin(ang)], axis=1).astype(BF16)


def _row(v):
    return v.reshape(1, -1)


def kernel(x, p, mix_pre_g, mix_post_g, ffn_pre_g, ffn_post_g, ev_w_in, ev_w_fourier, ev_v_ln_g, ev_v_ln_b, ev_w_spatial, ev_b_spatial, ev_w_out, od_w_in, od_conv_w, od_conv_b, od_ln_g, od_ln_b, od_sconv_w, od_w_out, ffn_w_up, ffn_conv_w, ffn_conv_b, ffn_w_down, ple_w_p, ple_gate_g, ple_w_g, ple_b_g):
    depth = mix_pre_g.shape[0]
    s = x.shape[1]
    ts = min(512, s)
    cs = _seq_dft_table(s)
    for i in range(depth):
        j = i // 2
        if i % 2 == 0:
            wf = _channel_dft_weights(ev_w_fourier[j], s)
            bs = jnp.broadcast_to(ev_b_spatial[j][:, :, None], (GROUPS, CHUNK, LANES))
            z, yb = _even_in(x, _row(mix_pre_g[i]), ev_w_in[j].astype(BF16), wf,
                             _row(ev_v_ln_g[j]), _row(ev_v_ln_b[j]),
                             ev_w_spatial[j].astype(BF16), bs, ts)
            ya = _seq_dft(cs, z, ts)
            x = _mix_out(x, ya, yb, ev_w_out[j].astype(BF16), _row(mix_post_g[i]), ts)
        else:
            x = _odd_mix(x, _row(mix_pre_g[i]), od_w_in[j].astype(BF16), od_conv_w[j], _row(od_conv_b[j]),
                         _row(od_ln_g[j]), _row(od_ln_b[j]), od_sconv_w[j], od_w_out[j].astype(BF16),
                         _row(mix_post_g[i]), ts)
        x = _ffn_ple(x, p, i, _row(ffn_pre_g[i]), ffn_w_up[i].astype(BF16), ffn_conv_w[i],
                     _row(ffn_conv_b[i]), ffn_w_down[i].astype(BF16), _row(ffn_post_g[i]),
                     _row(ple_gate_g[i]), ple_w_g[i].astype(BF16), _row(ple_b_g[i]),
                     ple_w_p[i].astype(BF16), ts)
    return x
```

```python
import functools
import math

import jax
import jax.numpy as jnp
import numpy as np
from jax import lax
from jax.experimental import pallas as pl
from jax.experimental.pallas import tpu as pltpu

EPS = 1e-6
LANES = 128
GROUPS = 4
BRANCH = GROUPS * LANES
CHUNK = 128
CONF_K = 31
CONF_HALO = 16
FFN_HALO = 8
V7X_VMEM_LIMIT = 56 * 1024 * 1024

F32 = jnp.float32
BF16 = jnp.bfloat16


def _rms(x, g):
    ms = jnp.mean(x * x, axis=-1, keepdims=True)
    return x * lax.rsqrt(ms + EPS) * g


def _group_layernorm(x, g, b):
    outs = []
    for k in range(GROUPS):
        sl = slice(k * LANES, (k + 1) * LANES)
        xs = x[:, sl]
        mu = jnp.mean(xs, axis=-1, keepdims=True)
        xc = xs - mu
        var = jnp.mean(xc * xc, axis=-1, keepdims=True)
        outs.append(xc * lax.rsqrt(var + EPS) * g[:, sl] + b[:, sl])
    return jnp.concatenate(outs, axis=-1)


def _dot(a, b):
    return jnp.dot(a, b, preferred_element_type=F32)


def _resident(shape):
    nd = len(shape)
    return pl.BlockSpec(shape, lambda *_: (0,) * nd, pipeline_mode=pl.Buffered(1))


def _params(n_grid_axes):
    return pltpu.CompilerParams(
        dimension_semantics=("arbitrary",) * n_grid_axes,
        vmem_limit_bytes=V7X_VMEM_LIMIT,
    )


def _even_in_kernel(x_ref, g_ref, win_ref, wf_ref, lng_ref, lnb_ref, ws_ref, bs_ref,
                    z_ref, yb_ref, perm_ref, *, ts):
    h = _rms(x_ref[0], g_ref[...]).astype(BF16)
    z = _dot(h, win_ref[...])
    za = z[:, :BRANCH].astype(BF16)
    n_blk = ts // DFT_N2
    for k in range(GROUPS):
        t = _dot(za[:, k * LANES:(k + 1) * LANES], wf_ref[k])
        for l in range(2):
            lanes = slice(l * LANES, (l + 1) * LANES)
            slab = perm_ref.at[2 * k + l]
            for j in range(n_blk):
                slab[pl.ds(ROW_PITCH * PERM_PITCH * j, DFT_N2, stride=ROW_PITCH), :] = (
                    t[j * DFT_N2:(j + 1) * DFT_N2, lanes])
            for n2 in range(DFT_N2):
                rows = slab[pl.ds(ROW_PITCH * n2, n_blk, stride=ROW_PITCH * PERM_PITCH), :]
                z_ref[0, n2, :, k * 2 * LANES + l * LANES:k * 2 * LANES + (l + 1) * LANES] = rows.astype(BF16)
    zuv = jax.nn.gelu(z[:, BRANCH:])
    zu = zuv[:, :BRANCH]
    vn = _group_layernorm(zuv[:, BRANCH:], lng_ref[...], lnb_ref[...]).astype(BF16)
    for k in range(GROUPS):
        sl = slice(k * LANES, (k + 1) * LANES)
        wsk = ws_ref[k]
        bsk = bs_ref[k]
        for c in range(ts // CHUNK):
            rows = slice(c * CHUNK, (c + 1) * CHUNK)
            sv = _dot(wsk, vn[rows, sl]) + bsk
            yb_ref[0, rows, sl] = (zu[rows, sl] * sv).astype(BF16)


def _even_in(x, g, w_in, wf, ln_g, ln_b, ws, bs, ts):
    b, s, d = x.shape
    return pl.pallas_call(
        functools.partial(_even_in_kernel, ts=ts),
        grid=(b, s // ts),
        in_specs=[
            pl.BlockSpec((1, ts, d), lambda i, j: (i, j, 0)),
            _resident(g.shape), _resident(w_in.shape), _resident(wf.shape),
            _resident(ln_g.shape), _resident(ln_b.shape), _resident(ws.shape), _resident(bs.shape),
        ],
        out_specs=[
            pl.BlockSpec((1, DFT_N2, ts // DFT_N2, 2 * BRANCH), lambda i, j: (i, 0, j, 0)),
            pl.BlockSpec((1, ts, BRANCH), lambda i, j: (i, j, 0)),
        ],
        out_shape=[
            jax.ShapeDtypeStruct((b, DFT_N2, s // DFT_N2, 2 * BRANCH), BF16),
            jax.ShapeDtypeStruct((b, s, BRANCH), BF16),
        ],
        scratch_shapes=[
            pltpu.VMEM((2 * GROUPS, ROW_PITCH * PERM_PITCH * (ts // DFT_N2), LANES), F32),
        ],
        compiler_params=_params(2),
        name="even_in",
    )(x, g, w_in, wf, ln_g, ln_b, ws, bs)


DFT_N1 = 256
DFT_N2 = 16
PERM_PITCH = 17
DFT_ROWS = 16


def _dft16_real(br, bi):
    u = [[None] * 4 for _ in range(4)]
    for nb in range(4):
        x0r, x1r, x2r, x3r = br[nb], br[4 + nb], br[8 + nb], br[12 + nb]
        x0i, x1i, x2i, x3i = bi[nb], bi[4 + nb], bi[8 + nb], bi[12 + nb]
        t0r, t0i = x0r + x2r, x0i + x2i
        t1r, t1i = x0r - x2r, x0i - x2i
        t2r, t2i = x1r + x3r, x1i + x3i
        t3r, t3i = x1r - x3r, x1i - x3i
        u[0][nb] = (t0r + t2r, t0i + t2i)
        u[2][nb] = (t0r - t2r, t0i - t2i)
        u[1][nb] = (t1r + t3i, t1i - t3r)
        u[3][nb] = (t1r - t3i, t1i + t3r)
    out = [None] * 16
    for ka in range(4):
        vr, vi = [], []
        for nb in range(4):
            a, b = u[ka][nb]
            e = (nb * ka) % 16
            wr, wi = math.cos(-2.0 * math.pi * e / 16), math.sin(-2.0 * math.pi * e / 16)
            if e == 0:
                vr.append(a)
                vi.append(b)
            elif e == 4:
                vr.append(b)
                vi.append(-a)
            else:
                vr.append(a * wr - b * wi)
                vi.append(a * wi + b * wr if nb in (1, 3) else None)
        s0, s1, s2, s3 = vr[0] + vr[2], vr[0] - vr[2], vr[1] + vr[3], vi[1] - vi[3]
        out[ka] = s0 + s2
        out[ka + 4] = s1 + s3
        out[ka + 8] = s0 - s2
        out[ka + 12] = s1 - s3
    return out


def _seq_dft_kernel(tab_ref, z_ref, o_ref, bre_ref, bim_ref):
    for n2 in range(DFT_N2):
        t = _dot(tab_ref[n2], z_ref[0, n2])
        bre_ref[n2] = t[:DFT_N1, :LANES] - t[DFT_N1:, LANES:]
        bim_ref[n2] = t[:DFT_N1, LANES:] + t[DFT_N1:, :LANES]
    for r in range(DFT_N1 // DFT_ROWS):
        rows = slice(r * DFT_ROWS, (r + 1) * DFT_ROWS)
        y = _dft16_real([bre_ref[n2, rows, :] for n2 in range(DFT_N2)],
                        [bim_ref[n2, rows, :] for n2 in range(DFT_N2)])
        for k2 in range(DFT_N2):
            o_ref[0, k2 * DFT_N1 + r * DFT_ROWS:k2 * DFT_N1 + (r + 1) * DFT_ROWS, :] = y[k2].astype(BF16)


def _seq_dft(tab, z):
    b, n2, n1, _ = z.shape
    assert (n2, n1) == (DFT_N2, DFT_N1), "the factored sequence DFT is written for S = 4096"
    return pl.pallas_call(
        _seq_dft_kernel,
        grid=(b, GROUPS),
        in_specs=[
            _resident(tab.shape),
            pl.BlockSpec((1, n2, n1, 2 * LANES), lambda i, g: (i, 0, 0, g)),
        ],
        out_specs=pl.BlockSpec((1, n1 * n2, LANES), lambda i, g: (i, 0, g)),
        out_shape=jax.ShapeDtypeStruct((b, n1 * n2, BRANCH), BF16),
        scratch_shapes=[
            pltpu.VMEM((n2, n1, LANES), F32),
            pltpu.VMEM((n2, n1, LANES), F32),
        ],
        compiler_params=_params(2),
        name="seq_dft",
    )(tab, z)


def _mix_out_kernel(x_ref, ya_ref, yb_ref, wout_ref, g_ref, o_ref):
    y = _dot(ya_ref[0], wout_ref[:BRANCH, :]) + _dot(yb_ref[0], wout_ref[BRANCH:, :])
    o_ref[0] = x_ref[0] + _rms(y, g_ref[...])


def _mix_out(x, ya, yb, w_out, g, ts):
    b, s, d = x.shape
    return pl.pallas_call(
        _mix_out_kernel,
        grid=(b, s // ts),
        in_specs=[
            pl.BlockSpec((1, ts, d), lambda i, j: (i, j, 0)),
            pl.BlockSpec((1, ts, BRANCH), lambda i, j: (i, j, 0)),
            pl.BlockSpec((1, ts, BRANCH), lambda i, j: (i, j, 0)),
            _resident(w_out.shape), _resident(g.shape),
        ],
        out_specs=pl.BlockSpec((1, ts, d), lambda i, j: (i, j, 0)),
        out_shape=jax.ShapeDtypeStruct(x.shape, F32),
        compiler_params=_params(2),
        name="mix_out",
    )(x, ya, yb, w_out, g)


def _halo_specs(ts, halo, d, s):
    per = ts // halo
    last = s // halo - 1
    return [
        pl.BlockSpec((1, ts, d), lambda i, j: (i, j, 0)),
        pl.BlockSpec((1, halo, d), lambda i, j: (i, jnp.maximum(j * per - 1, 0), 0)),
        pl.BlockSpec((1, halo, d), lambda i, j: (i, jnp.minimum((j + 1) * per, last), 0)),
    ]


def _normed_rows(x_ref, prev_ref, next_ref, g):
    j = pl.program_id(1)
    h_main = _rms(x_ref[0], g).astype(BF16)
    h_prev = jnp.where(j > 0, _rms(prev_ref[0], g), 0.0)
    h_next = jnp.where(j < pl.num_programs(1) - 1, _rms(next_ref[0], g), 0.0)
    h_halo = jnp.concatenate([h_prev, h_next], axis=0).astype(BF16)
    return jnp.concatenate([h_main, h_halo], axis=0)


ROW_PITCH = 2


def _stage_slab(slab_ref, v, ts, halo):
    slab_ref[pl.ds(0, halo, stride=ROW_PITCH), :] = v[ts:ts + halo, :]
    slab_ref[pl.ds(ROW_PITCH * halo, ts, stride=ROW_PITCH), :] = v[0:ts, :]
    slab_ref[pl.ds(ROW_PITCH * (halo + ts), halo, stride=ROW_PITCH), :] = v[ts + halo:ts + 2 * halo, :]


def _slab_rows(slab_ref, first, n):
    return slab_ref[pl.ds(ROW_PITCH * first, n, stride=ROW_PITCH), :]


CONV_ROWS = 64


def _odd_mix_kernel(x_ref, prev_ref, next_ref, pre_g_ref, win_ref, cw_ref, cb_ref, lng_ref, lnb_ref,
                    sw_ref, wout_ref, post_g_ref, o_ref, gslab_ref, mslab_ref, *, ts):
    halo = CONF_HALO
    pair = 2 * LANES
    h = _normed_rows(x_ref, prev_ref, next_ref, pre_g_ref[...])

    def proj(first_col, width):
        return _dot(h, win_ref[:, first_col:first_col + width])

    def glu_conv_norm(a, gate, first_group):
        glu = a * jax.nn.sigmoid(gate)
        outs = []
        for l in range(2):
            k = first_group + l
            sl = slice(k * LANES, (k + 1) * LANES)
            slab = gslab_ref.at[k]
            _stage_slab(slab, glu[:, l * LANES:(l + 1) * LANES], ts, halo)
            first = halo - CONF_K // 2
            blocks = []
            for r in range(ts // CONV_ROWS):
                r0 = r * CONV_ROWS + first
                acc = _slab_rows(slab, r0, CONV_ROWS) * cw_ref[0:1, sl]
                for t in range(1, CONF_K):
                    acc = acc + _slab_rows(slab, r0 + t, CONV_ROWS) * cw_ref[t:t + 1, sl]
                blocks.append(acc)
            c = jnp.concatenate(blocks, axis=0) + cb_ref[:, sl]
            mu = jnp.mean(c, axis=-1, keepdims=True)
            cc = c - mu
            var = jnp.mean(cc * cc, axis=-1, keepdims=True)
            outs.append(jax.nn.silu(cc * lax.rsqrt(var + EPS) * lng_ref[:, sl] + lnb_ref[:, sl]))
        return jnp.concatenate(outs, axis=-1).astype(BF16)

    a0, g0 = proj(0, pair), proj(BRANCH, pair)
    a1, g1 = proj(pair, pair), proj(BRANCH + pair, pair)
    yc0 = glu_conv_norm(a0, g0, 0)
    bg = proj(2 * BRANCH, BRANCH)[:ts]
    cg = proj(3 * BRANCH, BRANCH)
    xin = proj(4 * BRANCH, BRANCH)
    yc1 = glu_conv_norm(a1, g1, 2)
    y = _dot(yc0, wout_ref[0:pair, :]) + _dot(yc1, wout_ref[pair:BRANCH, :])
    m = cg * xin
    yd = []
    for k in range(GROUPS):
        sl = slice(k * LANES, (k + 1) * LANES)
        slab = mslab_ref.at[k]
        _stage_slab(slab, m[:, sl], ts, halo)
        conv = (_slab_rows(slab, halo - 1, ts) * sw_ref[0:1, sl] + m[0:ts, sl] * sw_ref[1:2, sl]
                + _slab_rows(slab, halo + 1, ts) * sw_ref[2:3, sl])
        yd.append(bg[:, sl] * conv)
    y = y + _dot(jnp.concatenate(yd, axis=-1).astype(BF16), wout_ref[BRANCH:, :])
    o_ref[0] = x_ref[0] + _rms(y, post_g_ref[...])


def _odd_mix(x, pre_g, w_in, conv_w, conv_b, ln_g, ln_b, sconv_w, w_out, post_g, ts):
    b, s, d = x.shape
    halo = CONF_HALO
    consts = (pre_g, w_in, conv_w, conv_b, ln_g, ln_b, sconv_w, w_out, post_g)
    return pl.pallas_call(
        functools.partial(_odd_mix_kernel, ts=ts),
        grid=(b, s // ts),
        in_specs=_halo_specs(ts, halo, d, s) + [_resident(c.shape) for c in consts],
        out_specs=pl.BlockSpec((1, ts, d), lambda i, j: (i, j, 0)),
        out_shape=jax.ShapeDtypeStruct(x.shape, F32),
        scratch_shapes=[
            pltpu.VMEM((GROUPS, ROW_PITCH * (ts + 2 * halo), LANES), F32),
            pltpu.VMEM((GROUPS, ROW_PITCH * (ts + 2 * halo), LANES), F32),
        ],
        compiler_params=_params(2),
        name="odd_mix",
    )(x, x, x, *consts)


FF_CHUNK = 256


def _ffn_ple_kernel(x_ref, prev_ref, next_ref, p_ref, pre_g_ref, wup_ref, cw_ref, cb_ref, wdown_ref,
                    post_g_ref, gate_g_ref, wg_ref, bg_ref, wp_ref, o_ref, ext_ref, act_ref, *, ts, d_ff):
    halo = FFN_HALO
    n_slab = FF_CHUNK // LANES
    h = _normed_rows(x_ref, prev_ref, next_ref, pre_g_ref[...])
    n_chunks = d_ff // FF_CHUNK

    def up_proj(c):
        return [_dot(h, wup_ref[:, base + c * FF_CHUNK:base + (c + 1) * FF_CHUNK]) for base in (0, d_ff)]

    z_next = up_proj(0)
    for c in range(n_chunks):
        z_cur = z_next
        if c + 1 < n_chunks:
            z_next = up_proj(c + 1)
        halves = []
        for half, base in enumerate((0, d_ff)):
            cols = slice(base + c * FF_CHUNK, base + (c + 1) * FF_CHUNK)
            zc = z_cur[half]
            w = cw_ref[:, cols]
            bias = cb_ref[:, cols]
            outs = []
            for l in range(n_slab):
                lanes = slice(l * LANES, (l + 1) * LANES)
                slab = ext_ref.at[c % 2, half * n_slab + l]
                _stage_slab(slab, zc[:, lanes], ts, halo)
                outs.append(_slab_rows(slab, halo - 1, ts) * w[0:1, lanes]
                            + zc[0:ts, lanes] * w[1:2, lanes]
                            + _slab_rows(slab, halo + 1, ts) * w[2:3, lanes]
                            + bias[:, lanes])
            halves.append(jnp.concatenate(outs, axis=-1))
        act_ref[:, c * FF_CHUNK:(c + 1) * FF_CHUNK] = (jax.nn.gelu(halves[0]) * halves[1]).astype(BF16)
    f = _dot(act_ref[...], wdown_ref[...])
    x2 =x_ref[0] + _rms(f, post_g_ref[...])
    hg = _rms(x2, gate_g_ref[...]).astype(BF16)
    gate = jax.nn.sigmoid(_dot(hg, wg_ref[...]) + bg_ref[...])
    o_ref[0] = x2 + gate * _dot(p_ref[0, 0].astype(BF16), wp_ref[...])


def _ffn_ple(x, p, layer, pre_g, w_up, conv_w, conv_b, w_down, post_g, gate_g, w_g, b_g, w_p, ts):
    b, s, d = x.shape
    halo = FFN_HALO
    d_ff = w_down.shape[0]
    consts = (pre_g, w_up, conv_w, conv_b, w_down, post_g, gate_g, w_g, b_g, w_p)
    return pl.pallas_call(
        functools.partial(_ffn_ple_kernel, ts=ts, d_ff=d_ff),
        grid=(b, s // ts),
        in_specs=_halo_specs(ts, halo, d, s)
        + [pl.BlockSpec((1, 1, ts, p.shape[-1]), lambda i, j: (layer, i, j, 0))]
        + [_resident(c.shape) for c in consts],
        out_specs=pl.BlockSpec((1, ts, d), lambda i, j: (i, j, 0)),
        out_shape=jax.ShapeDtypeStruct(x.shape, F32),
        scratch_shapes=[
            pltpu.VMEM((2, 2 * FF_CHUNK // LANES, ROW_PITCH * (ts + 2 * halo), LANES), F32),
            pltpu.VMEM((ts, d_ff), BF16),
        ],
        compiler_params=_params(2),
        name="ffn_ple",
    )(x, x, x, p, *consts)


def _channel_dft_weights(w_f, s):
    n = np.arange(LANES)
    ang = 2.0 * np.pi * ((n[:, None] * n[None, :]) % LANES) / LANES
    scale = 1.0 / math.sqrt(float(s) * LANES)
    cos = jnp.asarray(np.cos(ang) * scale, F32)
    sin = jnp.asarray(-np.sin(ang) * scale, F32)
    re = jnp.einsum("cd,gde->gce", cos, w_f, precision=lax.Precision.HIGHEST)
    im = jnp.einsum("cd,gde->gce", sin, w_f, precision=lax.Precision.HIGHEST)
    return jnp.concatenate([re, im], axis=-1).astype(BF16)


def _seq_dft_table():
    shape = (DFT_N2, DFT_N1, DFT_N1)
    n2 = lax.broadcasted_iota(jnp.int32, shape, 0)
    k1 = lax.broadcasted_iota(jnp.int32, shape, 1)
    n1 = lax.broadcasted_iota(jnp.int32, shape, 2)
    s = DFT_N1 * DFT_N2
    ang = (((n2 + DFT_N2 * n1) * k1) % s).astype(F32) * (-2.0 * math.pi / s)
    return jnp.concatenate([jnp.cos(ang), jnp.sin(ang)], axis=1).astype(BF16)


def _row(v):
    return v.reshape(1, -1)


def kernel(x, p, mix_pre_g, mix_post_g, ffn_pre_g, ffn_post_g, ev_w_in, ev_w_fourier, ev_v_ln_g, ev_v_ln_b, ev_w_spatial, ev_b_spatial, ev_w_out, od_w_in, od_conv_w, od_conv_b, od_ln_g, od_ln_b, od_sconv_w, od_w_out, ffn_w_up, ffn_conv_w, ffn_conv_b, ffn_w_down, ple_w_p, ple_gate_g, ple_w_g, ple_b_g):
    depth = mix_pre_g.shape[0]
    s = x.shape[1]
    ts = min(512, s)
    tab = _seq_dft_table()
    for i in range(depth):
        j = i // 2
        if i % 2 == 0:
            wf = _channel_dft_weights(ev_w_fourier[j], s)
            bs = jnp.broadcast_to(ev_b_spatial[j][:, :, None], (GROUPS, CHUNK, LANES))
            z, yb = _even_in(x, _row(mix_pre_g[i]), ev_w_in[j].astype(BF16), wf,
                             _row(ev_v_ln_g[j]), _row(ev_v_ln_b[j]),
                             ev_w_spatial[j].astype(BF16), bs, ts)
            ya = _seq_dft(tab, z)
            x = _mix_out(x, ya, yb, ev_w_out[j].astype(BF16), _row(mix_post_g[i]), ts)
        else:
            x = _odd_mix(x, _row(mix_pre_g[i]), od_w_in[j].astype(BF16), od_conv_w[j], _row(od_conv_b[j]),
                         _row(od_ln_g[j]), _row(od_ln_b[j]), od_sconv_w[j], od_w_out[j].astype(BF16),
                         _row(mix_post_g[i]), ts)
        x = _ffn_ple(x, p, i, _row(ffn_pre_g[i]), ffn_w_up[i].astype(BF16), ffn_conv_w[i],
                     _row(ffn_conv_b[i]), ffn_w_down[i].astype(BF16), _row(ffn_post_g[i]),
                     _row(ple_gate_g[i]), ple_w_g[i].astype(BF16), _row(ple_b_g[i]),
                     ple_w_p[i].astype(BF16), ts)
    return x
```

```python
import functools
import math

import jax
import jax.numpy as jnp
from jax import lax
from jax.experimental import pallas as pl
from jax.experimental.pallas import tpu as pltpu

EPS = 1e-6
LANES = 128
BF16_ROWS = 16
GROUPS = 4
BRANCH = GROUPS * LANES
CHUNK = 128
CONF_K = 31
CONF_HALO = 16
FFN_HALO = 8
V7X_VMEM_LIMIT = 56 * 1024 * 1024

F32 = jnp.float32
BF16 = jnp.bfloat16


def _rms(x, g):
    ms = jnp.mean(x * x, axis=-1, keepdims=True)
    return x * lax.rsqrt(ms + EPS) * g


def _layernorm(x, g, b):
    mu = jnp.mean(x, axis=-1, keepdims=True)
    xc = x - mu
    var = jnp.mean(xc * xc, axis=-1, keepdims=True)
    return xc * lax.rsqrt(var + EPS) * g + b


def _dot(a, b):
    return jnp.dot(a, b, preferred_element_type=F32)


def _resident(shape):
    nd = len(shape)
    return pl.BlockSpec(shape, lambda *_: (0,) * nd, pipeline_mode=pl.Buffered(1))


def _layer(arr, layer):
    nd = arr.ndim - 1
    return pl.BlockSpec((None,) + arr.shape[1:], lambda *_: (layer,) + (0,) * nd, pipeline_mode=pl.Buffered(1))


def _params(n_grid_axes):
    return pltpu.CompilerParams(
        dimension_semantics=("arbitrary",) * n_grid_axes,
        vmem_limit_bytes=V7X_VMEM_LIMIT,
    )


ROW_PITCH = 2


def _stage_slab(slab_ref, v, ts, halo):
    slab_ref[pl.ds(0, halo, stride=ROW_PITCH), :] = v[ts:ts + halo, :]
    slab_ref[pl.ds(ROW_PITCH * halo, ts, stride=ROW_PITCH), :] = v[0:ts, :]
    slab_ref[pl.ds(ROW_PITCH * (halo + ts), halo, stride=ROW_PITCH), :] = v[ts + halo:ts + 2 * halo, :]


def _slab_rows(slab_ref, first, n):
    return slab_ref[pl.ds(ROW_PITCH * first, n, stride=ROW_PITCH), :]


def _halo_specs(ts, halo, width, s):
    per = ts // halo
    last = s // halo - 1
    return [
        pl.BlockSpec((1, ts, width), lambda i, j: (i, j, 0)),
        pl.BlockSpec((1, halo, width), lambda i, j: (i, jnp.maximum(j * per - 1, 0), 0)),
        pl.BlockSpec((1, halo, width), lambda i, j: (i, jnp.minimum((j + 1) * per, last), 0)),
    ]


def _mask_halo(v, ts, halo):
    j = pl.program_id(1)
    row = lax.broadcasted_iota(jnp.int32, (ts + 2 * halo, 1), 0)
    outside = ((row >= ts) & (row < ts + halo) & (j == 0)) | ((row >= ts + halo) & (j == pl.num_programs(1) - 1))
    return jnp.where(outside, 0.0, v)


DFT_N1 = 256
DFT_N2 = 16
PERM_PITCH = 17


def _even_in_kernel(x_ref, g_ref, win_ref, wf_ref, lng_ref, lnb_ref, ws_ref, bs_ref,
                    z_ref, yb_ref, perm_ref, *, ts):
    h = _rms(x_ref[0], g_ref[...]).astype(BF16)
    z = _dot(h, win_ref[...])
    za = z[:, :BRANCH].astype(BF16)
    n_blk = ts // DFT_N2
    for k in range(GROUPS):
        t = _dot(za[:, k * LANES:(k + 1) * LANES], wf_ref[k])
        for l in range(2):
            lanes = slice(l * LANES, (l + 1) * LANES)
            slab = perm_ref.at[2 * k + l]
            for j in range(n_blk):
                slab[pl.ds(ROW_PITCH * PERM_PITCH * j, DFT_N2, stride=ROW_PITCH), :] = (
                    t[j * DFT_N2:(j + 1) * DFT_N2, lanes])
            for n2 in range(DFT_N2):
                rows = slab[pl.ds(ROW_PITCH * n2, n_blk, stride=ROW_PITCH * PERM_PITCH), :]
                z_ref[0, n2, :, k * 2 * LANES + l * LANES:k * 2 * LANES + (l + 1) * LANES] = rows.astype(BF16)
    zuv = jax.nn.gelu(z[:, BRANCH:])
    zu = zuv[:, :BRANCH]
    zv = zuv[:, BRANCH:]
    for k in range(GROUPS):
        sl = slice(k * LANES, (k + 1) * LANES)
        vn = _layernorm(zv[:, sl], lng_ref[:, sl], lnb_ref[:, sl]).astype(BF16)
        wsk = ws_ref[k]
        bsk = bs_ref[k]
        for c in range(ts // CHUNK):
            rows = slice(c * CHUNK, (c + 1) * CHUNK)
            sv = _dot(wsk, vn[rows, :]) + bsk
            yb_ref[0, rows, sl] = (zu[rows, sl] * sv).astype(BF16)


def _even_in(x, layer, j, pre_g, w_in, wf, ln_g, ln_b, ws, bs, ts):
    b, s, d = x.shape
    return pl.pallas_call(
        functools.partial(_even_in_kernel, ts=ts),
        grid=(b, s // ts),
        in_specs=[
            pl.BlockSpec((1, ts, d), lambda i, t: (i, t, 0)),
            _layer(pre_g, layer), _layer(w_in, j), _resident(wf.shape),
            _layer(ln_g, j), _layer(ln_b, j), _layer(ws, j), _resident(bs.shape),
        ],
        out_specs=[
            pl.BlockSpec((1, DFT_N2, ts // DFT_N2, 2 * BRANCH), lambda i, t: (i, 0, t, 0)),
            pl.BlockSpec((1, ts, BRANCH), lambda i, t: (i, t, 0)),
        ],
        out_shape=[
            jax.ShapeDtypeStruct((b, DFT_N2, s // DFT_N2, 2 * BRANCH), BF16),
            jax.ShapeDtypeStruct((b, s, BRANCH), BF16),
        ],
        scratch_shapes=[
            pltpu.VMEM((2 * GROUPS, ROW_PITCH * PERM_PITCH * (ts // DFT_N2), LANES), F32),
        ],
        compiler_params=_params(2),
        name="even_in",
    )(x, pre_g, w_in, wf, ln_g, ln_b, ws, bs)


DFT_ROWS = 16


def _dft16_real(br, bi):
    u = [[None] * 4 for _ in range(4)]
    for nb in range(4):
        x0r, x1r, x2r, x3r = br[nb], br[4 + nb], br[8 + nb], br[12 + nb]
        x0i, x1i, x2i, x3i = bi[nb], bi[4 + nb], bi[8 + nb], bi[12 + nb]
        t0r, t0i = x0r + x2r, x0i + x2i
        t1r, t1i = x0r - x2r, x0i - x2i
        t2r, t2i = x1r + x3r, x1i + x3i
        t3r, t3i = x1r - x3r, x1i - x3i
        u[0][nb] = (t0r + t2r, t0i + t2i)
        u[2][nb] = (t0r - t2r, t0i - t2i)
        u[1][nb] = (t1r + t3i, t1i - t3r)
        u[3][nb] = (t1r - t3i, t1i + t3r)
    out = [None] * 16
    for ka in range(4):
        vr, vi = [], []
        for nb in range(4):
            a, b = u[ka][nb]
            e = (nb * ka) % 16
            wr, wi = math.cos(-2.0 * math.pi * e / 16), math.sin(-2.0 * math.pi * e / 16)
            if e == 0:
                vr.append(a)
                vi.append(b)
            elif e == 4:
                vr.append(b)
                vi.append(-a)
            else:
                vr.append(a * wr - b * wi)
                vi.append(a * wi + b * wr if nb in (1, 3) else None)
        s0, s1, s2, s3 = vr[0] + vr[2], vr[0] - vr[2], vr[1] + vr[3], vi[1] - vi[3]
        out[ka] = s0 + s2
        out[ka + 4] = s1 + s3
        out[ka + 8] = s0 - s2
        out[ka + 12] = s1 - s3
    return out


def _seq_dft_kernel(tab_ref, z_ref, o_ref, bre_ref, bim_ref):
    for n2 in range(DFT_N2):
        t = _dot(tab_ref[n2], z_ref[0, n2])
        bre_ref[n2] = t[:DFT_N1, :LANES] - t[DFT_N1:, LANES:]
        bim_ref[n2] = t[:DFT_N1, LANES:] + t[DFT_N1:, :LANES]
    for r in range(DFT_N1 // DFT_ROWS):
        rows = slice(r * DFT_ROWS, (r + 1) * DFT_ROWS)
        y = _dft16_real([bre_ref[n2, rows, :] for n2 in range(DFT_N2)],
                        [bim_ref[n2, rows, :] for n2 in range(DFT_N2)])
        for k2 in range(DFT_N2):
            o_ref[0, k2 * DFT_N1 + r * DFT_ROWS:k2 * DFT_N1 + (r + 1) * DFT_ROWS, :] = y[k2].astype(BF16)


def _seq_dft(tab, z):
    b, n2, n1, _ = z.shape
    assert (n2, n1) == (DFT_N2, DFT_N1), "the factored sequence DFT is written for S = 4096"
    return pl.pallas_call(
        _seq_dft_kernel,
        grid=(b, GROUPS),
        in_specs=[
            _resident(tab.shape),
            pl.BlockSpec((1, n2, n1, 2 * LANES), lambda i, g: (i, 0, 0, g)),
        ],
        out_specs=pl.BlockSpec((1, n1 * n2, LANES), lambda i, g: (i, 0, g)),
        out_shape=jax.ShapeDtypeStruct((b, n1 * n2, BRANCH), BF16),
        scratch_shapes=[
            pltpu.VMEM((n2, n1, LANES), F32),
            pltpu.VMEM((n2, n1, LANES), F32),
        ],
        compiler_params=_params(2),
        name="seq_dft",
    )(tab, z)


CONV_ROWS = 64


def _odd_mix_kernel(x_ref, prev_ref, next_ref, pre_g_ref, win_ref, cw_ref, cb_ref, lng_ref, lnb_ref,
                    sw_ref, wout_ref, post_g_ref, o_ref, gslab_ref, mslab_ref, *, ts):
    halo = CONF_HALO
    pair = 2 * LANES
    x_all = jnp.concatenate([x_ref[0], prev_ref[0], next_ref[0]], axis=0)
    h = _mask_halo(_rms(x_all, pre_g_ref[...]), ts, halo).astype(BF16)

    def proj(first_col, width):
        return _dot(h, win_ref[:, first_col:first_col + width])

    def glu_conv_norm(a, gate, first_group):
        glu = a * jax.nn.sigmoid(gate)
        outs = []
        for l in range(2):
            k = first_group + l
            sl = slice(k * LANES, (k + 1) * LANES)
            slab = gslab_ref.at[k]
            _stage_slab(slab, glu[:, l * LANES:(l + 1) * LANES], ts, halo)
            first = halo - CONF_K // 2
            blocks = []
            for r in range(ts // CONV_ROWS):
                r0 = r * CONV_ROWS + first
                acc = _slab_rows(slab, r0, CONV_ROWS) * cw_ref[0:1, sl]
                for t in range(1, CONF_K):
                    acc = acc + _slab_rows(slab, r0 + t, CONV_ROWS) * cw_ref[t:t + 1, sl]
                blocks.append(acc)
            c = jnp.concatenate(blocks, axis=0) + cb_ref[:, sl]
            outs.append(jax.nn.silu(_layernorm(c, lng_ref[:, sl], lnb_ref[:, sl])))
        return jnp.concatenate(outs, axis=-1).astype(BF16)

    a0, g0 = proj(0, pair), proj(BRANCH, pair)
    a1, g1 = proj(pair, pair), proj(BRANCH + pair, pair)
    yc0 = glu_conv_norm(a0, g0, 0)
    bg = proj(2 * BRANCH, BRANCH)[:ts]
    cg = proj(3 * BRANCH, BRANCH)
    xin = proj(4 * BRANCH, BRANCH)
    yc1 = glu_conv_norm(a1, g1, 2)
    y = _dot(yc0, wout_ref[0:pair, :]) + _dot(yc1, wout_ref[pair:BRANCH, :])
    m = cg * xin
    yd = []
    for k in range(GROUPS):
        sl = slice(k * LANES, (k + 1) * LANES)
        slab = mslab_ref.at[k]
        _stage_slab(slab, m[:, sl], ts, halo)
        conv = (_slab_rows(slab, halo - 1, ts) * sw_ref[0:1, sl] + m[0:ts, sl] * sw_ref[1:2, sl]
                + _slab_rows(slab, halo + 1, ts) * sw_ref[2:3, sl])
        yd.append(bg[:, sl] * conv)
    y = y + _dot(jnp.concatenate(yd, axis=-1).astype(BF16), wout_ref[BRANCH:, :])
    o_ref[0] = x_ref[0] + _rms(y, post_g_ref[...])


def _odd_mix(x, layer, j, pre_g, w_in, conv_w, conv_b, ln_g, ln_b, sconv_w, w_out, post_g, ts):
    b, s, d = x.shape
    halo = CONF_HALO
    return pl.pallas_call(
        functools.partial(_odd_mix_kernel, ts=ts),
        grid=(b, s // ts),
        in_specs=_halo_specs(ts, halo, d, s) + [
            _layer(pre_g, layer), _layer(w_in, j), _layer(conv_w, j), _layer(conv_b, j), _layer(ln_g, j),
            _layer(ln_b, j), _layer(sconv_w, j), _layer(w_out, j), _layer(post_g, layer)],
        out_specs=pl.BlockSpec((1, ts, d), lambda i, t: (i, t, 0)),
        out_shape=jax.ShapeDtypeStruct(x.shape, F32),
        scratch_shapes=[
            pltpu.VMEM((GROUPS, ROW_PITCH * (ts + 2 * halo), LANES), F32),
            pltpu.VMEM((GROUPS, ROW_PITCH * (ts + 2 * halo), LANES), F32),
        ],
        compiler_params=_params(2),
        name="odd_mix",
    )(x, x, x, pre_g, w_in, conv_w, conv_b, ln_g, ln_b, sconv_w, w_out, post_g)


FF_CHUNK = 256


def _ffn_ple_tail(x1_all, p_ref, pre_g_ref, wup_ref, cw_ref, cb_ref, wdown_ref, post_g_ref, gate_g_ref,
                  wg_ref, bg_ref, wp_ref, o_ref, ext_ref, act_ref, ts):
    halo = FFN_HALO
    d_ff = wdown_ref.shape[0]
    n_slab = FF_CHUNK // LANES
    n_chunks = d_ff // FF_CHUNK
    h = _mask_halo(_rms(x1_all, pre_g_ref[...]), ts, halo).astype(BF16)

    def up_proj(c):
        return [_dot(h, wup_ref[:, base + c * FF_CHUNK:base + (c + 1) * FF_CHUNK]) for base in (0, d_ff)]

    z_next = up_proj(0)
    for c in range(n_chunks):
        z_cur = z_next
        if c + 1 < n_chunks:
            z_next = up_proj(c + 1)
        halves = []
        for half, base in enumerate((0, d_ff)):
            cols = slice(base + c * FF_CHUNK, base + (c + 1) * FF_CHUNK)
            zc = z_cur[half]
            w = cw_ref[:, cols]
            bias = cb_ref[:, cols]
            outs = []
            for l in range(n_slab):
                lanes = slice(l * LANES, (l + 1) * LANES)
                slab = ext_ref.at[c % 2, half * n_slab + l]
                _stage_slab(slab, zc[:, lanes], ts, halo)
                outs.append(_slab_rows(slab, halo - 1, ts) * w[0:1, lanes]
                            + zc[0:ts, lanes] * w[1:2, lanes]
                            + _slab_rows(slab, halo + 1, ts) * w[2:3, lanes]
                            + bias[:, lanes])
            halves.append(jnp.concatenate(outs, axis=-1))
        act_ref[:, c * FF_CHUNK:(c + 1) * FF_CHUNK] = (jax.nn.gelu(halves[0]) * halves[1]).astype(BF16)
    f = _dot(act_ref[...], wdown_ref[...])
    x2 = x1_all[:ts] + _rms(f, post_g_ref[...])
    hg = _rms(x2, gate_g_ref[...]).astype(BF16)
    gate = jax.nn.sigmoid(_dot(hg, wg_ref[...]) + bg_ref[...])
    o_ref[0] = x2 + gate * _dot(p_ref[0].astype(BF16), wp_ref[...])


def _ffn_ple_kernel(x_ref, prev_ref, next_ref, p_ref, *rest, ts):
    x1_all = jnp.concatenate([x_ref[0], prev_ref[0], next_ref[0]], axis=0)
    _ffn_ple_tail(x1_all, p_ref, *rest, ts)


def _even_tail_kernel(x_ref, prev_ref, next_ref, ya_ref, ya_prev_ref, ya_next_ref, yb_ref, yb_prev_ref,
                      yb_next_ref, wout_ref, mix_g_ref, p_ref, *rest, ts):
    lo = BF16_ROWS - FFN_HALO

    def with_halo(main_ref, before_ref, after_ref):
        halo = jnp.concatenate([before_ref[0].astype(F32)[lo:], after_ref[0].astype(F32)[:FFN_HALO]], axis=0)
        return jnp.concatenate([main_ref[0], halo.astype(BF16)], axis=0)

    x_all = jnp.concatenate([x_ref[0], prev_ref[0], next_ref[0]], axis=0)
    y = (_dot(with_halo(ya_ref, ya_prev_ref, ya_next_ref), wout_ref[:BRANCH, :])
         + _dot(with_halo(yb_ref, yb_prev_ref, yb_next_ref), wout_ref[BRANCH:, :]))
    _ffn_ple_tail(x_all + _rms(y, mix_g_ref[...]), p_ref, *rest, ts)


def _ffn_call(kernel_fn, name, front_specs, front_args, x, p, layer, pre_g, w_up, conv_w, conv_b, w_down,
              post_g, gate_g, w_g, b_g, w_p, ts):
    b, s, d = x.shape
    halo = FFN_HALO
    d_ff = w_down.shape[1]
    tail = (pre_g, w_up, conv_w, conv_b, w_down, post_g, gate_g, w_g, b_g, w_p)
    return pl.pallas_call(
        functools.partial(kernel_fn, ts=ts),
        grid=(b, s // ts),
        in_specs=_halo_specs(ts, halo, d, s) + front_specs
        + [pl.BlockSpec((None, 1, ts, p.shape[-1]), lambda i, t: (layer, i, t, 0))]
        + [_layer(c, layer) for c in tail],
        out_specs=pl.BlockSpec((1, ts, d), lambda i, t: (i, t, 0)),
        out_shape=jax.ShapeDtypeStruct(x.shape, F32),
        scratch_shapes=[
            pltpu.VMEM((2, 2 * FF_CHUNK // LANES, ROW_PITCH * (ts + 2 * halo), LANES), F32),
            pltpu.VMEM((ts, d_ff), BF16),
        ],
        compiler_params=_params(2),
        name=name,
    )(x, x, x, *front_args, p, *tail)


def _ffn_ple(x, p, layer, *tail, ts):
    return _ffn_call(_ffn_ple_kernel, "ffn_ple", [], [], x, p, layer, *tail, ts)


def _even_tail(x, ya, yb, j, w_out, mix_g, p, layer, *tail, ts):
    s = x.shape[1]
    front_specs = (_halo_specs(ts, BF16_ROWS, BRANCH, s) + _halo_specs(ts, BF16_ROWS, BRANCH, s)
                   + [_layer(w_out, j), _layer(mix_g, layer)])
    return _ffn_call(_even_tail_kernel, "even_tail", front_specs, [ya, ya, ya, yb, yb, yb, w_out, mix_g],
                     x, p, layer, *tail, ts)


def _channel_dft_weights(w_f, s):
    n = lax.broadcasted_iota(jnp.int32, (LANES, LANES), 0) * lax.broadcasted_iota(jnp.int32, (LANES, LANES), 1)
    ang = (n % LANES).astype(F32) * (2.0 * math.pi / LANES)
    scale = 1.0 / math.sqrt(float(s) * LANES)
    re = jnp.einsum("cd,gde->gce", jnp.cos(ang) * scale, w_f, precision=lax.Precision.HIGHEST)
    im = jnp.einsum("cd,gde->gce", -jnp.sin(ang) * scale, w_f, precision=lax.Precision.HIGHEST)
    return jnp.concatenate([re, im], axis=-1).astype(BF16)


def _seq_dft_table():
    shape = (DFT_N2, DFT_N1, DFT_N1)
    n2 = lax.broadcasted_iota(jnp.int32, shape, 0)
    k1 = lax.broadcasted_iota(jnp.int32, shape, 1)
    n1 = lax.broadcasted_iota(jnp.int32, shape, 2)
    s = DFT_N1 * DFT_N2
    ang = (((n2 + DFT_N2 * n1) * k1) % s).astype(F32) * (-2.0 * math.pi / s)
    return jnp.concatenate([jnp.cos(ang), jnp.sin(ang)], axis=1).astype(BF16)


def _rows(v):
    return v[:, None, :]


def kernel(x, p, mix_pre_g, mix_post_g, ffn_pre_g, ffn_post_g, ev_w_in, ev_w_fourier, ev_v_ln_g, ev_v_ln_b, ev_w_spatial, ev_b_spatial, ev_w_out, od_w_in, od_conv_w, od_conv_b, od_ln_g, od_ln_b, od_sconv_w, od_w_out, ffn_w_up, ffn_conv_w, ffn_conv_b, ffn_w_down, ple_w_p, ple_gate_g, ple_w_g, ple_b_g):
    depth = mix_pre_g.shape[0]
    s = x.shape[1]
    ts = min(512, s)
    tab = _seq_dft_table()
    mix_pre_g, mix_post_g = _rows(mix_pre_g), _rows(mix_post_g)
    ffn_tail = (_rows(ffn_pre_g), ffn_w_up.astype(BF16), ffn_conv_w, _rows(ffn_conv_b), ffn_w_down.astype(BF16),
                _rows(ffn_post_g), _rows(ple_gate_g), ple_w_g.astype(BF16), _rows(ple_b_g), ple_w_p.astype(BF16))
    ev_w_in, ev_w_spatial, ev_w_out = ev_w_in.astype(BF16), ev_w_spatial.astype(BF16), ev_w_out.astype(BF16)
    ev_v_ln_g, ev_v_ln_b = _rows(ev_v_ln_g), _rows(ev_v_ln_b)
    od_w_in, od_w_out = od_w_in.astype(BF16), od_w_out.astype(BF16)
    od_conv_b, od_ln_g, od_ln_b = _rows(od_conv_b), _rows(od_ln_g), _rows(od_ln_b)
    for i in range(depth):
        j = i // 2
        if i % 2 == 0:
            wf = _channel_dft_weights(ev_w_fourier[j], s)
            bs = jnp.broadcast_to(ev_b_spatial[j][:, :, None], (GROUPS, CHUNK, LANES))
            z, yb = _even_in(x, i, j, mix_pre_g, ev_w_in, wf, ev_v_ln_g, ev_v_ln_b, ev_w_spatial, bs, ts)
            ya = _seq_dft(tab, z)
            x = _even_tail(x, ya, yb, j, ev_w_out, mix_post_g, p, i, *ffn_tail, ts=ts)
        else:
            x = _odd_mix(x, i, j, mix_pre_g, od_w_in, od_conv_w, od_conv_b, od_ln_g, od_ln_b, od_sconv_w,
                         od_w_out, mix_post_g, ts)
            x = _ffn_ple(x, p, i, *ffn_tail, ts=ts)
    return x
```

```python
import functools
import math

import jax
import jax.numpy as jnp
from jax import lax
from jax.experimental import pallas as pl
from jax.experimental.pallas import tpu as pltpu

EPS = 1e-6
LANES = 128
BF16_ROWS = 16
GROUPS = 4
BRANCH = GROUPS * LANES
CHUNK = 128
CONF_K = 31
CONF_HALO = 16
FFN_HALO = 8
V7X_VMEM_LIMIT = 56 * 1024 * 1024

F32 = jnp.float32
BF16 = jnp.bfloat16


def _rms(x, g):
    ms = jnp.mean(x * x, axis=-1, keepdims=True)
    return x * lax.rsqrt(ms + EPS) * g


def _layernorm(x, g, b):
    mu = jnp.mean(x, axis=-1, keepdims=True)
    xc = x - mu
    var = jnp.mean(xc * xc, axis=-1, keepdims=True)
    return xc * lax.rsqrt(var + EPS) * g + b


def _dot(a, b):
    return jnp.dot(a, b, preferred_element_type=F32)


def _resident(shape):
    nd = len(shape)
    return pl.BlockSpec(shape, lambda *_: (0,) * nd, pipeline_mode=pl.Buffered(1))


def _layer(arr, layer):
    nd = arr.ndim - 1
    return pl.BlockSpec((None,) + arr.shape[1:], lambda *_: (layer,) + (0,) * nd, pipeline_mode=pl.Buffered(1))


def _params(n_grid_axes):
    return pltpu.CompilerParams(
        dimension_semantics=("arbitrary",) * n_grid_axes,
        vmem_limit_bytes=V7X_VMEM_LIMIT,
    )


ROW_PITCH = 2


def _stage_slab(slab_ref, v, ts, halo):
    slab_ref[pl.ds(0, halo, stride=ROW_PITCH), :] = v[ts:ts + halo, :]
    slab_ref[pl.ds(ROW_PITCH * halo, ts, stride=ROW_PITCH), :] = v[0:ts, :]
    slab_ref[pl.ds(ROW_PITCH * (halo + ts), halo, stride=ROW_PITCH), :] = v[ts + halo:ts + 2 * halo, :]


def _slab_rows(slab_ref, first, n):
    return slab_ref[pl.ds(ROW_PITCH * first, n, stride=ROW_PITCH), :]


def _halo_specs(ts, halo, width, s):
    per = ts // halo
    last = s // halo - 1
    return [
        pl.BlockSpec((1, ts, width), lambda i, j: (i, j, 0)),
        pl.BlockSpec((1, halo, width), lambda i, j: (i, jnp.maximum(j * per - 1, 0), 0)),
        pl.BlockSpec((1, halo, width), lambda i, j: (i, jnp.minimum((j + 1) * per, last), 0)),
    ]


def _mask_halo(v, ts, halo):
    j = pl.program_id(1)
    row = lax.broadcasted_iota(jnp.int32, (ts + 2 * halo, 1), 0)
    outside = ((row >= ts) & (row < ts + halo) & (j == 0)) | ((row >= ts + halo) & (j == pl.num_programs(1) - 1))
    return jnp.where(outside, 0.0, v)


DFT_N1 = 256
DFT_N2 = 16
PERM_PITCH = 17


def _even_in_kernel(x_ref, g_ref, win_ref, wf_ref, lng_ref, lnb_ref, ws_ref, bs_ref,
                    z_ref, yb_ref, perm_ref, *, ts):
    h = _rms(x_ref[0], g_ref[...]).astype(BF16)
    pair = 2 * LANES
    n_blk = ts // DFT_N2

    def in_proj(q):
        return [_dot(h, win_ref[:, base + q * pair:base + (q + 1) * pair]) for base in (0, BRANCH, 2 * BRANCH)]

    def fourier_channel_stage(za, k):
        t = _dot(za.astype(BF16), wf_ref[k])
        for l in range(2):
            lanes = slice(l * LANES, (l + 1) * LANES)
            slab = perm_ref.at[2 * k + l]
            for j in range(n_blk):
                slab[pl.ds(ROW_PITCH * PERM_PITCH * j, DFT_N2, stride=ROW_PITCH), :] = (
                    t[j * DFT_N2:(j + 1) * DFT_N2, lanes])
            for n2 in range(DFT_N2):
                rows = slab[pl.ds(ROW_PITCH * n2, n_blk, stride=ROW_PITCH * PERM_PITCH), :]
                z_ref[0, n2, :, k * 2 * LANES + l * LANES:k * 2 * LANES + (l + 1) * LANES] = rows.astype(BF16)

    def spatial_gating(zu, zv, k):
        sl = slice(k * LANES, (k + 1) * LANES)
        vn = _layernorm(jax.nn.gelu(zv), lng_ref[:, sl], lnb_ref[:, sl]).astype(BF16)
        zu = jax.nn.gelu(zu)
        for c in range(ts // CHUNK):
            rows = slice(c * CHUNK, (c + 1) * CHUNK)
            sv = _dot(ws_ref[k], vn[rows, :]) + bs_ref[k]
            yb_ref[0, rows, sl] = (zu[rows, :] * sv).astype(BF16)

    z_next = in_proj(0)
    for q in range(GROUPS // 2):
        za, zu, zv = z_next
        if q + 1 < GROUPS // 2:
            z_next = in_proj(q + 1)
        for l in range(2):
            lanes = slice(l * LANES, (l + 1) * LANES)
            fourier_channel_stage(za[:, lanes], 2 * q + l)
            spatial_gating(zu[:, lanes], zv[:, lanes], 2 * q + l)


def _even_in(x, layer, j, pre_g, w_in, wf, ln_g, ln_b, ws, bs, ts):
    b, s, d = x.shape
    return pl.pallas_call(
        functools.partial(_even_in_kernel, ts=ts),
        grid=(b, s // ts),
        in_specs=[
            pl.BlockSpec((1, ts, d), lambda i, t: (i, t, 0)),
            _layer(pre_g, layer), _layer(w_in, j), _resident(wf.shape),
            _layer(ln_g, j), _layer(ln_b, j), _layer(ws, j), _resident(bs.shape),
        ],
        out_specs=[
            pl.BlockSpec((1, DFT_N2, ts // DFT_N2, 2 * BRANCH), lambda i, t: (i, 0, t, 0)),
            pl.BlockSpec((1, ts, BRANCH), lambda i, t: (i, t, 0)),
        ],
        out_shape=[
            jax.ShapeDtypeStruct((b, DFT_N2, s // DFT_N2, 2 * BRANCH), BF16),
            jax.ShapeDtypeStruct((b, s, BRANCH), BF16),
        ],
        scratch_shapes=[
            pltpu.VMEM((2 * GROUPS, ROW_PITCH * PERM_PITCH * (ts // DFT_N2), LANES), F32),
        ],
        compiler_params=_params(2),
        name="even_in",
    )(x, pre_g, w_in, wf, ln_g, ln_b, ws, bs)


DFT_ROWS = 16


def _dft16_real(br, bi):
    u = [[None] * 4 for _ in range(4)]
    for nb in range(4):
        x0r, x1r, x2r, x3r = br[nb], br[4 + nb], br[8 + nb], br[12 + nb]
        x0i, x1i, x2i, x3i = bi[nb], bi[4 + nb], bi[8 + nb], bi[12 + nb]
        t0r, t0i = x0r + x2r, x0i + x2i
        t1r, t1i = x0r - x2r, x0i - x2i
        t2r, t2i = x1r + x3r, x1i + x3i
        t3r, t3i = x1r - x3r, x1i - x3i
        u[0][nb] = (t0r + t2r, t0i + t2i)
        u[2][nb] = (t0r - t2r, t0i - t2i)
        u[1][nb] = (t1r + t3i, t1i - t3r)
        u[3][nb] = (t1r - t3i, t1i + t3r)
    out = [None] * 16
    for ka in range(4):
        vr, vi = [], []
        for nb in range(4):
            a, b = u[ka][nb]
            e = (nb * ka) % 16
            wr, wi = math.cos(-2.0 * math.pi * e / 16), math.sin(-2.0 * math.pi * e / 16)
            if e == 0:
                vr.append(a)
                vi.append(b)
            elif e == 4:
                vr.append(b)
                vi.append(-a)
            else:
                vr.append(a * wr - b * wi)
                vi.append(a * wi + b * wr if nb in (1, 3) else None)
        s0, s1, s2, s3 = vr[0] + vr[2], vr[0] - vr[2], vr[1] + vr[3], vi[1] - vi[3]
        out[ka] = s0 + s2
        out[ka + 4] = s1 + s3
        out[ka + 8] = s0 - s2
        out[ka + 12] = s1 - s3
    return out


def _seq_dft_kernel(tab_ref, z_ref, o_ref, bre_ref, bim_ref):
    def matmul_stage(g):
        for n2 in range(DFT_N2):
            t = _dot(tab_ref[n2], z_ref[0, n2, :, g * 2 * LANES:(g + 1) * 2 * LANES])
            bre_ref[g % 2, n2] = t[:DFT_N1, :LANES] - t[DFT_N1:, LANES:]
            bim_ref[g % 2, n2] = t[:DFT_N1, LANES:] + t[DFT_N1:, :LANES]

    def slab_stage(g):
        for r in range(DFT_N1 // DFT_ROWS):
            rows = slice(r * DFT_ROWS, (r + 1) * DFT_ROWS)
            y = _dft16_real([bre_ref[g % 2, n2, rows, :] for n2 in range(DFT_N2)],
                            [bim_ref[g % 2, n2, rows, :] for n2 in range(DFT_N2)])
            for k2 in range(DFT_N2):
                o_ref[0, k2 * DFT_N1 + r * DFT_ROWS:k2 * DFT_N1 + (r + 1) * DFT_ROWS,
                      g * LANES:(g + 1) * LANES] = y[k2].astype(BF16)

    matmul_stage(0)
    for g in range(GROUPS):
        if g + 1 < GROUPS:
            matmul_stage(g + 1)
        slab_stage(g)


def _seq_dft(tab, z):
    b, n2, n1, width = z.shape
    assert (n2, n1) == (DFT_N2, DFT_N1), "the factored sequence DFT is written for S = 4096"
    return pl.pallas_call(
        _seq_dft_kernel,
        grid=(b,),
        in_specs=[
            _resident(tab.shape),
            pl.BlockSpec((1, n2, n1, width), lambda i: (i, 0, 0, 0)),
        ],
        out_specs=pl.BlockSpec((1, n1 * n2, BRANCH), lambda i: (i, 0, 0)),
        out_shape=jax.ShapeDtypeStruct((b, n1 * n2, BRANCH), BF16),
        scratch_shapes=[
            pltpu.VMEM((2, n2, n1, LANES), F32),
            pltpu.VMEM((2, n2, n1, LANES), F32),
        ],
        compiler_params=_params(1),
        name="seq_dft",
    )(tab, z)


CONV_ROWS = 64


def _odd_mix_kernel(x_ref, prev_ref, next_ref, pre_g_ref, win_ref, cw_ref, cb_ref, lng_ref, lnb_ref,
                    sw_ref, wout_ref, post_g_ref, o_ref, gslab_ref, mslab_ref, *, ts):
    halo = CONF_HALO
    pair = 2 * LANES
    x_all = jnp.concatenate([x_ref[0], prev_ref[0], next_ref[0]], axis=0)
    h = _mask_halo(_rms(x_all, pre_g_ref[...]), ts, halo).astype(BF16)

    def proj(first_col, width):
        return _dot(h, win_ref[:, first_col:first_col + width])

    def glu_conv_norm(a, gate, first_group):
        glu = a * jax.nn.sigmoid(gate)
        outs = []
        for l in range(2):
            k = first_group + l
            sl = slice(k * LANES, (k + 1) * LANES)
            slab = gslab_ref.at[k]
            _stage_slab(slab, glu[:, l * LANES:(l + 1) * LANES], ts, halo)
            first = halo - CONF_K // 2
            blocks = []
            for r in range(ts // CONV_ROWS):
                r0 = r * CONV_ROWS + first
                acc = _slab_rows(slab, r0, CONV_ROWS) * cw_ref[0:1, sl]
                for t in range(1, CONF_K):
                    acc = acc + _slab_rows(slab, r0 + t, CONV_ROWS) * cw_ref[t:t + 1, sl]
                blocks.append(acc)
            c = jnp.concatenate(blocks, axis=0) + cb_ref[:, sl]
            outs.append(jax.nn.silu(_layernorm(c, lng_ref[:, sl], lnb_ref[:, sl])))
        return jnp.concatenate(outs, axis=-1).astype(BF16)

    a0, g0 = proj(0, pair), proj(BRANCH, pair)
    a1, g1 = proj(pair, pair), proj(BRANCH + pair, pair)
    yc0 = glu_conv_norm(a0, g0, 0)
    bg = proj(2 * BRANCH, BRANCH)[:ts]
    cg = proj(3 * BRANCH, BRANCH)
    xin = proj(4 * BRANCH, BRANCH)
    yc1 = glu_conv_norm(a1, g1, 2)
    y = _dot(yc0, wout_ref[0:pair, :]) + _dot(yc1, wout_ref[pair:BRANCH, :])
    m = cg * xin
    yd = []
    for k in range(GROUPS):
        sl = slice(k * LANES, (k + 1) * LANES)
        slab = mslab_ref.at[k]
        _stage_slab(slab, m[:, sl], ts, halo)
        conv = (_slab_rows(slab, halo - 1, ts) * sw_ref[0:1, sl] + m[0:ts, sl] * sw_ref[1:2, sl]
                + _slab_rows(slab, halo + 1, ts) * sw_ref[2:3, sl])
        yd.append(bg[:, sl] * conv)
    y = y + _dot(jnp.concatenate(yd, axis=-1).astype(BF16), wout_ref[BRANCH:, :])
    o_ref[0] = x_ref[0] + _rms(y, post_g_ref[...])


def _odd_mix(x, layer, j, pre_g, w_in, conv_w, conv_b, ln_g, ln_b, sconv_w, w_out, post_g, ts):
    b, s, d = x.shape
    halo = CONF_HALO
    return pl.pallas_call(
        functools.partial(_odd_mix_kernel, ts=ts),
        grid=(b, s // ts),
        in_specs=_halo_specs(ts, halo, d, s) + [
            _layer(pre_g, layer), _layer(w_in, j), _layer(conv_w, j), _layer(conv_b, j), _layer(ln_g, j),
            _layer(ln_b, j), _layer(sconv_w, j), _layer(w_out, j), _layer(post_g, layer)],
        out_specs=pl.BlockSpec((1, ts, d), lambda i, t: (i, t, 0)),
        out_shape=jax.ShapeDtypeStruct(x.shape, F32),
        scratch_shapes=[
            pltpu.VMEM((GROUPS, ROW_PITCH * (ts + 2 * halo), LANES), F32),
            pltpu.VMEM((GROUPS, ROW_PITCH * (ts + 2 * halo), LANES), F32),
        ],
        compiler_params=_params(2),
        name="odd_mix",
    )(x, x, x, pre_g, w_in, conv_w, conv_b, ln_g, ln_b, sconv_w, w_out, post_g)


FF_CHUNK = 256


def _ffn_ple_tail(x1_all, p_ref, pre_g_ref, wup_ref, cw_ref, cb_ref, wdown_ref, post_g_ref, gate_g_ref,
                  wg_ref, bg_ref, wp_ref, o_ref, ext_ref, act_ref, ts):
    halo = FFN_HALO
    d_ff = wdown_ref.shape[0]
    n_slab = FF_CHUNK // LANES
    n_chunks = d_ff // FF_CHUNK
    h = _mask_halo(_rms(x1_all, pre_g_ref[...]), ts, halo).astype(BF16)

    def up_proj(c):
        return [_dot(h, wup_ref[:, base + c * FF_CHUNK:base + (c + 1) * FF_CHUNK]) for base in (0, d_ff)]

    z_next = up_proj(0)
    for c in range(n_chunks):
        z_cur = z_next
        if c + 1 < n_chunks:
            z_next = up_proj(c + 1)
        halves = []
        for half, base in enumerate((0, d_ff)):
            cols = slice(base + c * FF_CHUNK, base + (c + 1) * FF_CHUNK)
            zc = z_cur[half]
            w = cw_ref[:, cols]
            bias = cb_ref[:, cols]
            outs = []
            for l in range(n_slab):
                lanes = slice(l * LANES, (l + 1) * LANES)
                slab = ext_ref.at[c % 2, half * n_slab + l]
                _stage_slab(slab, zc[:, lanes], ts, halo)
                outs.append(_slab_rows(slab, halo - 1, ts) * w[0:1, lanes]
                            + zc[0:ts, lanes] * w[1:2, lanes]
                            + _slab_rows(slab, halo + 1, ts) * w[2:3, lanes]
                            + bias[:, lanes])
            halves.append(jnp.concatenate(outs, axis=-1))
        act_ref[:, c * FF_CHUNK:(c + 1) * FF_CHUNK] = (jax.nn.gelu(halves[0]) * halves[1]).astype(BF16)
    f = _dot(act_ref[...], wdown_ref[...])
    emb = _dot(p_ref[0].astype(BF16), wp_ref[...])
    x2 = x1_all[:ts] + _rms(f, post_g_ref[...])
    hg = _rms(x2, gate_g_ref[...]).astype(BF16)
    gate = jax.nn.sigmoid(_dot(hg, wg_ref[...]) + bg_ref[...])
    o_ref[0] = x2 + gate * emb


def _ffn_ple_kernel(x_ref, prev_ref, next_ref, p_ref, *rest, ts):
    x1_all = jnp.concatenate([x_ref[0], prev_ref[0], next_ref[0]], axis=0)
    _ffn_ple_tail(x1_all, p_ref, *rest, ts)


def _even_tail_kernel(x_ref, prev_ref, next_ref, ya_ref, ya_prev_ref, ya_next_ref, yb_ref, yb_prev_ref,
                      yb_next_ref, wout_ref, mix_g_ref, p_ref, *rest, ts):
    lo = BF16_ROWS - FFN_HALO

    def with_halo(main_ref, before_ref, after_ref):
        halo = jnp.concatenate([before_ref[0].astype(F32)[lo:], after_ref[0].astype(F32)[:FFN_HALO]], axis=0)
        return jnp.concatenate([main_ref[0], halo.astype(BF16)], axis=0)

    x_all = jnp.concatenate([x_ref[0], prev_ref[0], next_ref[0]], axis=0)
    y = (_dot(with_halo(ya_ref, ya_prev_ref, ya_next_ref), wout_ref[:BRANCH, :])
         + _dot(with_halo(yb_ref, yb_prev_ref, yb_next_ref), wout_ref[BRANCH:, :]))
    _ffn_ple_tail(x_all + _rms(y, mix_g_ref[...]), p_ref, *rest, ts)


def _ffn_call(kernel_fn, name, front_specs, front_args, x, p, layer, pre_g, w_up, conv_w, conv_b, w_down,
              post_g, gate_g, w_g, b_g, w_p, ts):
    b, s, d = x.shape
    halo = FFN_HALO
    d_ff = w_down.shape[1]
    tail = (pre_g, w_up, conv_w, conv_b, w_down, post_g, gate_g, w_g, b_g, w_p)
    return pl.pallas_call(
        functools.partial(kernel_fn, ts=ts),
        grid=(b, s // ts),
        in_specs=_halo_specs(ts, halo, d, s) + front_specs
        + [pl.BlockSpec((None, 1, ts, p.shape[-1]), lambda i, t: (layer, i, t, 0))]
        + [_layer(c, layer) for c in tail],
        out_specs=pl.BlockSpec((1, ts, d), lambda i, t: (i, t, 0)),
        out_shape=jax.ShapeDtypeStruct(x.shape, F32),
        scratch_shapes=[
            pltpu.VMEM((2, 2 * FF_CHUNK // LANES, ROW_PITCH * (ts + 2 * halo), LANES), F32),
            pltpu.VMEM((ts, d_ff), BF16),
        ],
        compiler_params=_params(2),
        name=name,
    )(x, x, x, *front_args, p, *tail)


def _ffn_ple(x, p, layer, *tail, ts):
    return _ffn_call(_ffn_ple_kernel, "ffn_ple", [], [], x, p, layer, *tail, ts)


def _even_tail(x, ya, yb, j, w_out, mix_g, p, layer, *tail, ts):
    s = x.shape[1]
    front_specs = (_halo_specs(ts, BF16_ROWS, BRANCH, s) + _halo_specs(ts, BF16_ROWS, BRANCH, s)
                   + [_layer(w_out, j), _layer(mix_g, layer)])
    return _ffn_call(_even_tail_kernel, "even_tail", front_specs, [ya, ya, ya, yb, yb, yb, w_out, mix_g],
                     x, p, layer, *tail, ts)


def _channel_dft_weights(w_f, s):
    n = lax.broadcasted_iota(jnp.int32, (LANES, LANES), 0) * lax.broadcasted_iota(jnp.int32, (LANES, LANES), 1)
    ang = (n % LANES).astype(F32) * (2.0 * math.pi / LANES)
    scale = 1.0 / math.sqrt(float(s) * LANES)
    re = jnp.einsum("cd,gde->gce", jnp.cos(ang) * scale, w_f, precision=lax.Precision.HIGHEST)
    im = jnp.einsum("cd,gde->gce", -jnp.sin(ang) * scale, w_f, precision=lax.Precision.HIGHEST)
    return jnp.concatenate([re, im], axis=-1).astype(BF16)


def _seq_dft_table():
    shape = (DFT_N2, DFT_N1, DFT_N1)
    n2 = lax.broadcasted_iota(jnp.int32, shape, 0)
    k1 = lax.broadcasted_iota(jnp.int32, shape, 1)
    n1 = lax.broadcasted_iota(jnp.int32, shape, 2)
    s = DFT_N1 * DFT_N2
    ang = (((n2 + DFT_N2 * n1) * k1) % s).astype(F32) * (-2.0 * math.pi / s)
    return jnp.concatenate([jnp.cos(ang), jnp.sin(ang)], axis=1).astype(BF16)


def _rows(v):
    return v[:, None, :]


def kernel(x, p, mix_pre_g, mix_post_g, ffn_pre_g, ffn_post_g, ev_w_in, ev_w_fourier, ev_v_ln_g, ev_v_ln_b, ev_w_spatial, ev_b_spatial, ev_w_out, od_w_in, od_conv_w, od_conv_b, od_ln_g, od_ln_b, od_sconv_w, od_w_out, ffn_w_up, ffn_conv_w, ffn_conv_b, ffn_w_down, ple_w_p, ple_gate_g, ple_w_g, ple_b_g):
    depth = mix_pre_g.shape[0]
    s = x.shape[1]
    ts = min(512, s)
    tab = _seq_dft_table()
    mix_pre_g, mix_post_g = _rows(mix_pre_g), _rows(mix_post_g)
    ffn_tail = (_rows(ffn_pre_g), ffn_w_up.astype(BF16), ffn_conv_w, _rows(ffn_conv_b), ffn_w_down.astype(BF16),
                _rows(ffn_post_g), _rows(ple_gate_g), ple_w_g.astype(BF16), _rows(ple_b_g), ple_w_p.astype(BF16))
    ev_w_in, ev_w_spatial, ev_w_out = ev_w_in.astype(BF16), ev_w_spatial.astype(BF16), ev_w_out.astype(BF16)
    ev_v_ln_g, ev_v_ln_b = _rows(ev_v_ln_g), _rows(ev_v_ln_b)
    od_w_in, od_w_out = od_w_in.astype(BF16), od_w_out.astype(BF16)
    od_conv_b, od_ln_g, od_ln_b = _rows(od_conv_b), _rows(od_ln_g), _rows(od_ln_b)
    for i in range(depth):
        j = i // 2
        if i % 2 == 0:
            wf = _channel_dft_weights(ev_w_fourier[j], s)
            bs = jnp.broadcast_to(ev_b_spatial[j][:, :, None], (GROUPS, CHUNK, LANES))
            z, yb = _even_in(x, i, j, mix_pre_g, ev_w_in, wf, ev_v_ln_g, ev_v_ln_b, ev_w_spatial, bs, ts)
            ya = _seq_dft(tab, z)
            x = _even_tail(x, ya, yb, j, ev_w_out, mix_post_g, p, i, *ffn_tail, ts=ts)
        else:
            x = _odd_mix(x, i, j, mix_pre_g, od_w_in, od_conv_w, od_conv_b, od_ln_g, od_ln_b, od_sconv_w,
                         od_w_out, mix_post_g, ts)
            x = _ffn_ple(x, p, i, *ffn_tail, ts=ts)
    return x
```

```python
import functools
import math

import jax
import jax.numpy as jnp
from jax import lax
from jax.experimental import pallas as pl
from jax.experimental.pallas import tpu as pltpu

EPS = 1e-6
LANES = 128
BF16_ROWS = 16
GROUPS = 4
BRANCH = GROUPS * LANES
CHUNK = 128
CONF_K = 31
CONF_HALO = 16
FFN_HALO = 8
V7X_VMEM_LIMIT = 56 * 1024 * 1024

F32 = jnp.float32
BF16 = jnp.bfloat16


def _rms(x, g):
    ms = jnp.mean(x * x, axis=-1, keepdims=True)
    return x * lax.rsqrt(ms + EPS) * g


def _layernorm(x, g, b):
    mu = jnp.mean(x, axis=-1, keepdims=True)
    xc = x - mu
    var = jnp.mean(xc * xc, axis=-1, keepdims=True)
    return xc * lax.rsqrt(var + EPS) * g + b


def _dot(a, b):
    return jnp.dot(a, b, preferred_element_type=F32)


def _resident(shape):
    nd = len(shape)
    return pl.BlockSpec(shape, lambda *_: (0,) * nd, pipeline_mode=pl.Buffered(1))


def _layer(arr, layer):
    nd = arr.ndim - 1
    return pl.BlockSpec((None,) + arr.shape[1:], lambda *_: (layer,) + (0,) * nd, pipeline_mode=pl.Buffered(1))


def _params(n_grid_axes):
    return pltpu.CompilerParams(
        dimension_semantics=("arbitrary",) * n_grid_axes,
        vmem_limit_bytes=V7X_VMEM_LIMIT,
    )


ROW_PITCH = 2


def _stage_slab(slab_ref, v, ts, halo):
    slab_ref[pl.ds(0, halo, stride=ROW_PITCH), :] = v[ts:ts + halo, :]
    slab_ref[pl.ds(ROW_PITCH * halo, ts, stride=ROW_PITCH), :] = v[0:ts, :]
    slab_ref[pl.ds(ROW_PITCH * (halo + ts), halo, stride=ROW_PITCH), :] = v[ts + halo:ts + 2 * halo, :]


def _slab_rows(slab_ref, first, n):
    return slab_ref[pl.ds(ROW_PITCH * first, n, stride=ROW_PITCH), :]


def _halo_specs(ts, halo, width, s):
    per = ts // halo
    last = s // halo - 1
    return [
        pl.BlockSpec((1, ts, width), lambda i, j: (i, j, 0)),
        pl.BlockSpec((1, halo, width), lambda i, j: (i, jnp.maximum(j * per - 1, 0), 0)),
        pl.BlockSpec((1, halo, width), lambda i, j: (i, jnp.minimum((j + 1) * per, last), 0)),
    ]


def _mask_halo(v, ts, halo):
    j = pl.program_id(1)
    row = lax.broadcasted_iota(jnp.int32, (ts + 2 * halo, 1), 0)
    outside = ((row >= ts) & (row < ts + halo) & (j == 0)) | ((row >= ts + halo) & (j == pl.num_programs(1) - 1))
    return jnp.where(outside, 0.0, v)


DFT_N1 = 256
DFT_N2 = 16
PERM_PITCH = 17


def _even_in_kernel(x_ref, g_ref, win_ref, wf_ref, lng_ref, lnb_ref, ws_ref, bs_ref,
                    z_ref, yb_ref, perm_ref, *, ts):
    h = _rms(x_ref[0], g_ref[...]).astype(BF16)
    pair = 2 * LANES
    n_blk = ts // DFT_N2

    def in_proj(q):
        return [_dot(h, win_ref[:, base + q * pair:base + (q + 1) * pair]) for base in (0, BRANCH, 2 * BRANCH)]

    def fourier_channel_stage(za, k):
        t = _dot(za.astype(BF16), wf_ref[k])
        for l in range(2):
            lanes = slice(l * LANES, (l + 1) * LANES)
            slab = perm_ref.at[2 * k + l]
            for j in range(n_blk):
                slab[pl.ds(ROW_PITCH * PERM_PITCH * j, DFT_N2, stride=ROW_PITCH), :] = (
                    t[j * DFT_N2:(j + 1) * DFT_N2, lanes])
            for n2 in range(DFT_N2):
                rows = slab[pl.ds(ROW_PITCH * n2, n_blk, stride=ROW_PITCH * PERM_PITCH), :]
                z_ref[0, n2, :, k * 2 * LANES + l * LANES:k * 2 * LANES + (l + 1) * LANES] = rows.astype(BF16)

    def spatial_gating(zu, zv, k):
        sl = slice(k * LANES, (k + 1) * LANES)
        vn = _layernorm(jax.nn.gelu(zv), lng_ref[:, sl], lnb_ref[:, sl]).astype(BF16)
        zu = jax.nn.gelu(zu)
        for c in range(ts // CHUNK):
            rows = slice(c * CHUNK, (c + 1) * CHUNK)
            sv = _dot(ws_ref[k], vn[rows, :]) + bs_ref[k]
            yb_ref[0, rows, sl] = (zu[rows, :] * sv).astype(BF16)

    z_next = in_proj(0)
    for q in range(GROUPS // 2):
        za, zu, zv = z_next
        if q + 1 < GROUPS // 2:
            z_next = in_proj(q + 1)
        for l in range(2):
            lanes = slice(l * LANES, (l + 1) * LANES)
            fourier_channel_stage(za[:, lanes], 2 * q + l)
            spatial_gating(zu[:, lanes], zv[:, lanes], 2 * q + l)


def _even_in(x, layer, j, pre_g, w_in, wf, ln_g, ln_b, ws, bs, ts):
    b, s, d = x.shape
    return pl.pallas_call(
        functools.partial(_even_in_kernel, ts=ts),
        grid=(b, s // ts),
        in_specs=[
            pl.BlockSpec((1, ts, d), lambda i, t: (i, t, 0)),
            _layer(pre_g, layer), _layer(w_in, j), _resident(wf.shape),
            _layer(ln_g, j), _layer(ln_b, j), _layer(ws, j), _resident(bs.shape),
        ],
        out_specs=[
            pl.BlockSpec((1, DFT_N2, ts // DFT_N2, 2 * BRANCH), lambda i, t: (i, 0, t, 0)),
            pl.BlockSpec((1, ts, BRANCH), lambda i, t: (i, t, 0)),
        ],
        out_shape=[
            jax.ShapeDtypeStruct((b, DFT_N2, s // DFT_N2, 2 * BRANCH), BF16),
            jax.ShapeDtypeStruct((b, s, BRANCH), BF16),
        ],
        scratch_shapes=[
            pltpu.VMEM((2 * GROUPS, ROW_PITCH * PERM_PITCH * (ts // DFT_N2), LANES), F32),
        ],
        compiler_params=_params(2),
        name="even_in",
    )(x, pre_g, w_in, wf, ln_g, ln_b, ws, bs)


DFT_ROWS = 16


def _dft16_real(br, bi):
    u = [[None] * 4 for _ in range(4)]
    for nb in range(4):
        x0r, x1r, x2r, x3r = br[nb], br[4 + nb], br[8 + nb], br[12 + nb]
        x0i, x1i, x2i, x3i = bi[nb], bi[4 + nb], bi[8 + nb], bi[12 + nb]
        t0r, t0i = x0r + x2r, x0i + x2i
        t1r, t1i = x0r - x2r, x0i - x2i
        t2r, t2i = x1r + x3r, x1i + x3i
        t3r, t3i = x1r - x3r, x1i - x3i
        u[0][nb] = (t0r + t2r, t0i + t2i)
        u[2][nb] = (t0r - t2r, t0i - t2i)
        u[1][nb] = (t1r + t3i, t1i - t3r)
        u[3][nb] = (t1r - t3i, t1i + t3r)
    out = [None] * 16
    for ka in range(4):
        vr, vi = [], []
        for nb in range(4):
            a, b = u[ka][nb]
            e = (nb * ka) % 16
            wr, wi = math.cos(-2.0 * math.pi * e / 16), math.sin(-2.0 * math.pi * e / 16)
            if e == 0:
                vr.append(a)
                vi.append(b)
            elif e == 4:
                vr.append(b)
                vi.append(-a)
            else:
                vr.append(a * wr - b * wi)
                vi.append(a * wi + b * wr if nb in (1, 3) else None)
        s0, s1, s2, s3 = vr[0] + vr[2], vr[0] - vr[2], vr[1] + vr[3], vi[1] - vi[3]
        out[ka] = s0 + s2
        out[ka + 4] = s1 + s3
        out[ka + 8] = s0 - s2
        out[ka + 12] = s1 - s3
    return out


def _seq_dft_kernel(tab_ref, z_ref, o_ref, bre_ref, bim_ref):
    def matmul_stage(g):
        for n2 in range(DFT_N2):
            t = _dot(tab_ref[n2], z_ref[0, n2, :, g * 2 * LANES:(g + 1) * 2 * LANES])
            bre_ref[g % 2, n2] = t[:DFT_N1, :LANES] - t[DFT_N1:, LANES:]
            bim_ref[g % 2, n2] = t[:DFT_N1, LANES:] + t[DFT_N1:, :LANES]

    def slab_stage(g):
        for r in range(DFT_N1 // DFT_ROWS):
            rows = slice(r * DFT_ROWS, (r + 1) * DFT_ROWS)
            y = _dft16_real([bre_ref[g % 2, n2, rows, :] for n2 in range(DFT_N2)],
                            [bim_ref[g % 2, n2, rows, :] for n2 in range(DFT_N2)])
            for k2 in range(DFT_N2):
                o_ref[0, k2 * DFT_N1 + r * DFT_ROWS:k2 * DFT_N1 + (r + 1) * DFT_ROWS,
                      g * LANES:(g + 1) * LANES] = y[k2].astype(BF16)

    matmul_stage(0)
    for g in range(GROUPS):
        if g + 1 < GROUPS:
            matmul_stage(g + 1)
        slab_stage(g)


def _seq_dft(tab, z):
    b, n2, n1, width = z.shape
    assert (n2, n1) == (DFT_N2, DFT_N1), "the factored sequence DFT is written for S = 4096"
    return pl.pallas_call(
        _seq_dft_kernel,
        grid=(b,),
        in_specs=[
            _resident(tab.shape),
            pl.BlockSpec((1, n2, n1, width), lambda i: (i, 0, 0, 0)),
        ],
        out_specs=pl.BlockSpec((1, n1 * n2, BRANCH), lambda i: (i, 0, 0)),
        out_shape=jax.ShapeDtypeStruct((b, n1 * n2, BRANCH), BF16),
        scratch_shapes=[
            pltpu.VMEM((2, n2, n1, LANES), F32),
            pltpu.VMEM((2, n2, n1, LANES), F32),
        ],
        compiler_params=_params(1),
        name="seq_dft",
    )(tab, z)


CONV_ROWS = 64


def _odd_mix_kernel(x_ref, prev_ref, next_ref, pre_g_ref, win_ref, cw_ref, cb_ref, lng_ref, lnb_ref,
                    sw_ref, wout_ref, post_g_ref, o_ref, gslab_ref, mslab_ref, *, ts):
    halo = CONF_HALO
    pair = 2 * LANES
    x_all = jnp.concatenate([x_ref[0], prev_ref[0], next_ref[0]], axis=0)
    h = _mask_halo(_rms(x_all, pre_g_ref[...]), ts, halo).astype(BF16)

    def proj(first_col, width):
        return _dot(h, win_ref[:, first_col:first_col + width])

    def glu_conv_norm(a, gate, first_group):
        glu = a * jax.nn.sigmoid(gate)
        outs = []
        for l in range(2):
            k = first_group + l
            sl = slice(k * LANES, (k + 1) * LANES)
            slab = gslab_ref.at[k]
            _stage_slab(slab, glu[:, l * LANES:(l + 1) * LANES], ts, halo)
            first = halo - CONF_K // 2
            blocks = []
            for r in range(ts // CONV_ROWS):
                r0 = r * CONV_ROWS + first
                acc = _slab_rows(slab, r0, CONV_ROWS) * cw_ref[0:1, sl]
                for t in range(1, CONF_K):
                    acc = acc + _slab_rows(slab, r0 + t, CONV_ROWS) * cw_ref[t:t + 1, sl]
                blocks.append(acc)
            c = jnp.concatenate(blocks, axis=0) + cb_ref[:, sl]
            outs.append(jax.nn.silu(_layernorm(c, lng_ref[:, sl], lnb_ref[:, sl])))
        return jnp.concatenate(outs, axis=-1).astype(BF16)

    a0, g0 = proj(0, pair), proj(BRANCH, pair)
    a1, g1 = proj(pair, pair), proj(BRANCH + pair, pair)
    yc0 = glu_conv_norm(a0, g0, 0)
    bg = proj(2 * BRANCH, BRANCH)[:ts]
    cg = proj(3 * BRANCH, BRANCH)
    xin = proj(4 * BRANCH, BRANCH)
    yc1 = glu_conv_norm(a1, g1, 2)
    y = _dot(yc0, wout_ref[0:pair, :]) + _dot(yc1, wout_ref[pair:BRANCH, :])
    m = cg * xin
    yd = []
    for k in range(GROUPS):
        sl = slice(k * LANES, (k + 1) * LANES)
        slab = mslab_ref.at[k]
        _stage_slab(slab, m[:, sl], ts, halo)
        conv = (_slab_rows(slab, halo - 1, ts) * sw_ref[0:1, sl] + m[0:ts, sl] * sw_ref[1:2, sl]
                + _slab_rows(slab, halo + 1, ts) * sw_ref[2:3, sl])
        yd.append(bg[:, sl] * conv)
    y = y + _dot(jnp.concatenate(yd, axis=-1).astype(BF16), wout_ref[BRANCH:, :])
    o_ref[0] = x_ref[0] + _rms(y, post_g_ref[...])


def _odd_mix(x, layer, j, pre_g, w_in, conv_w, conv_b, ln_g, ln_b, sconv_w, w_out, post_g, ts):
    b, s, d = x.shape
    halo = CONF_HALO
    return pl.pallas_call(
        functools.partial(_odd_mix_kernel, ts=ts),
        grid=(b, s // ts),
        in_specs=_halo_specs(ts, halo, d, s) + [
            _layer(pre_g, layer), _layer(w_in, j), _layer(conv_w, j), _layer(conv_b, j), _layer(ln_g, j),
            _layer(ln_b, j), _layer(sconv_w, j), _layer(w_out, j), _layer(post_g, layer)],
        out_specs=pl.BlockSpec((1, ts, d), lambda i, t: (i, t, 0)),
        out_shape=jax.ShapeDtypeStruct(x.shape, F32),
        scratch_shapes=[
            pltpu.VMEM((GROUPS, ROW_PITCH * (ts + 2 * halo), LANES), F32),
            pltpu.VMEM((GROUPS, ROW_PITCH * (ts + 2 * halo), LANES), F32),
        ],
        compiler_params=_params(2),
        name="odd_mix",
    )(x, x, x, pre_g, w_in, conv_w, conv_b, ln_g, ln_b, sconv_w, w_out, post_g)


FF_CHUNK = 256


def _ffn_ple_tail(x1_all, p_ref, pre_g_ref, wup_ref, cw_ref, cb_ref, wdown_ref, post_g_ref, gate_g_ref,
                  wg_ref, bg_ref, wp_ref, o_ref, ext_ref, act_ref, ts):
    halo = FFN_HALO
    d_ff = wdown_ref.shape[0]
    n_slab = FF_CHUNK // LANES
    n_chunks = d_ff // FF_CHUNK
    h = _mask_halo(_rms(x1_all, pre_g_ref[...]), ts, halo).astype(BF16)

    def up_proj(c):
        return [_dot(h, wup_ref[:, base + c * FF_CHUNK:base + (c + 1) * FF_CHUNK]) for base in (0, d_ff)]

    z_next = up_proj(0)
    for c in range(n_chunks):
        z_cur = z_next
        if c + 1 < n_chunks:
            z_next = up_proj(c + 1)
        halves = []
        for half, base in enumerate((0, d_ff)):
            cols = slice(base + c * FF_CHUNK, base + (c + 1) * FF_CHUNK)
            zc = z_cur[half]
            w = cw_ref[:, cols]
            bias = cb_ref[:, cols]
            outs = []
            for l in range(n_slab):
                lanes = slice(l * LANES, (l + 1) * LANES)
                slab = ext_ref.at[c % 2, half * n_slab + l]
                _stage_slab(slab, zc[:, lanes], ts, halo)
                outs.append(_slab_rows(slab, halo - 1, ts) * w[0:1, lanes]
                            + zc[0:ts, lanes] * w[1:2, lanes]
                            + _slab_rows(slab, halo + 1, ts) * w[2:3, lanes]
                            + bias[:, lanes])
            halves.append(jnp.concatenate(outs, axis=-1))
        act_ref[:, c * FF_CHUNK:(c + 1) * FF_CHUNK] = (jax.nn.gelu(halves[0]) * halves[1]).astype(BF16)
    f = _dot(act_ref[...], wdown_ref[...])
    emb = _dot(p_ref[0].astype(BF16), wp_ref[...])
    x2 = x1_all[:ts] + _rms(f, post_g_ref[...])
    hg = _rms(x2, gate_g_ref[...]).astype(BF16)
    gate = jax.nn.sigmoid(_dot(hg, wg_ref[...]) + bg_ref[...])
    o_ref[0] = x2 + gate * emb


def _ffn_ple_kernel(x_ref, prev_ref, next_ref, p_ref, *rest, ts):
    x1_all = jnp.concatenate([x_ref[0], prev_ref[0], next_ref[0]], axis=0)
    _ffn_ple_tail(x1_all, p_ref, *rest, ts)


def _even_tail_kernel(x_ref, prev_ref, next_ref, ya_ref, ya_prev_ref, ya_next_ref, yb_ref, yb_prev_ref,
                      yb_next_ref, wout_ref, mix_g_ref, p_ref, *rest, ts):
    lo = BF16_ROWS - FFN_HALO

    def with_halo(main_ref, before_ref, after_ref):
        halo = jnp.concatenate([before_ref[0].astype(F32)[lo:], after_ref[0].astype(F32)[:FFN_HALO]], axis=0)
        return jnp.concatenate([main_ref[0], halo.astype(BF16)], axis=0)

    x_all = jnp.concatenate([x_ref[0], prev_ref[0], next_ref[0]], axis=0)
    y = (_dot(with_halo(ya_ref, ya_prev_ref, ya_next_ref), wout_ref[:BRANCH, :])
         + _dot(with_halo(yb_ref, yb_prev_ref, yb_next_ref), wout_ref[BRANCH:, :]))
    _ffn_ple_tail(x_all + _rms(y, mix_g_ref[...]), p_ref, *rest, ts)


def _ffn_call(kernel_fn, name, front_specs, front_args, x, p, layer, pre_g, w_up, conv_w, conv_b, w_down,
              post_g, gate_g, w_g, b_g, w_p, ts):
    b, s, d = x.shape
    halo = FFN_HALO
    d_ff = w_down.shape[1]
    tail = (pre_g, w_up, conv_w, conv_b, w_down, post_g, gate_g, w_g, b_g, w_p)
    return pl.pallas_call(
        functools.partial(kernel_fn, ts=ts),
        grid=(b, s // ts),
        in_specs=_halo_specs(ts, halo, d, s) + front_specs
        + [pl.BlockSpec((None, 1, ts, p.shape[-1]), lambda i, t: (layer, i, t, 0))]
        + [_layer(c, layer) for c in tail],
        out_specs=pl.BlockSpec((1, ts, d), lambda i, t: (i, t, 0)),
        out_shape=jax.ShapeDtypeStruct(x.shape, F32),
        scratch_shapes=[
            pltpu.VMEM((2, 2 * FF_CHUNK // LANES, ROW_PITCH * (ts + 2 * halo), LANES), F32),
            pltpu.VMEM((ts, d_ff), BF16),
        ],
        compiler_params=_params(2),
        name=name,
    )(x, x, x, *front_args, p, *tail)


def _ffn_ple(x, p, layer, *tail, ts):
    return _ffn_call(_ffn_ple_kernel, "ffn_ple", [], [], x, p, layer, *tail, ts)


def _even_tail(x, ya, yb, j, w_out, mix_g, p, layer, *tail, ts):
    s = x.shape[1]
    front_specs = (_halo_specs(ts, BF16_ROWS, BRANCH, s) + _halo_specs(ts, BF16_ROWS, BRANCH, s)
                   + [_layer(w_out, j), _layer(mix_g, layer)])
    return _ffn_call(_even_tail_kernel, "even_tail", front_specs, [ya, ya, ya, yb, yb, yb, w_out, mix_g],
                     x, p, layer, *tail, ts)


def _channel_dft_kernel(cs_ref, wf_ref, o_ref):
    for g in range(GROUPS):
        for l in range(2):
            o_ref[g, :, l * LANES:(l + 1) * LANES] = jnp.dot(
                cs_ref[l], wf_ref[g], precision=lax.Precision.HIGHEST, preferred_element_type=F32).astype(BF16)


def _channel_dft_weights(w_f, s):
    n = lax.broadcasted_iota(jnp.int32, (LANES, LANES), 0) * lax.broadcasted_iota(jnp.int32, (LANES, LANES), 1)
    ang = (n % LANES).astype(F32) * (2.0 * math.pi / LANES)
    scale = 1.0 / math.sqrt(float(s) * LANES)
    cs = jnp.stack([jnp.cos(ang) * scale, -jnp.sin(ang) * scale])
    return pl.pallas_call(
        _channel_dft_kernel,
        out_shape=jax.ShapeDtypeStruct((GROUPS, LANES, 2 * LANES), BF16),
        name="channel_dft_weights",
    )(cs, w_f)


def _seq_dft_table():
    shape = (DFT_N2, DFT_N1, DFT_N1)
    n2 = lax.broadcasted_iota(jnp.int32, shape, 0)
    k1 = lax.broadcasted_iota(jnp.int32, shape, 1)
    n1 = lax.broadcasted_iota(jnp.int32, shape, 2)
    s = DFT_N1 * DFT_N2
    ang = (((n2 + DFT_N2 * n1) * k1) % s).astype(F32) * (-2.0 * math.pi / s)
    return jnp.concatenate([jnp.cos(ang), jnp.sin(ang)], axis=1).astype(BF16)


def _rows(v):
    return v[:, None, :]


def kernel(x, p, mix_pre_g, mix_post_g, ffn_pre_g, ffn_post_g, ev_w_in, ev_w_fourier, ev_v_ln_g, ev_v_ln_b, ev_w_spatial, ev_b_spatial, ev_w_out, od_w_in, od_conv_w, od_conv_b, od_ln_g, od_ln_b, od_sconv_w, od_w_out, ffn_w_up, ffn_conv_w, ffn_conv_b, ffn_w_down, ple_w_p, ple_gate_g, ple_w_g, ple_b_g):
    depth = mix_pre_g.shape[0]
    s = x.shape[1]
    ts = min(512, s)
    tab = _seq_dft_table()
    mix_pre_g, mix_post_g = _rows(mix_pre_g), _rows(mix_post_g)
    ffn_tail = (_rows(ffn_pre_g), ffn_w_up.astype(BF16), ffn_conv_w, _rows(ffn_conv_b), ffn_w_down.astype(BF16),
                _rows(ffn_post_g), _rows(ple_gate_g), ple_w_g.astype(BF16), _rows(ple_b_g), ple_w_p.astype(BF16))
    ev_w_in, ev_w_spatial, ev_w_out = ev_w_in.astype(BF16), ev_w_spatial.astype(BF16), ev_w_out.astype(BF16)
    ev_v_ln_g, ev_v_ln_b = _rows(ev_v_ln_g), _rows(ev_v_ln_b)
    od_w_in, od_w_out = od_w_in.astype(BF16), od_w_out.astype(BF16)
    od_conv_b, od_ln_g, od_ln_b = _rows(od_conv_b), _rows(od_ln_g), _rows(od_ln_b)
    for i in range(depth):
        j = i // 2
        if i % 2 == 0:
            wf = _channel_dft_weights(ev_w_fourier[j], s)
            bs = jnp.broadcast_to(ev_b_spatial[j][:, :, None], (GROUPS, CHUNK, LANES))
            z, yb = _even_in(x, i, j, mix_pre_g, ev_w_in, wf, ev_v_ln_g, ev_v_ln_b, ev_w_spatial, bs, ts)
            ya = _seq_dft(tab, z)
            x = _even_tail(x, ya, yb, j, ev_w_out, mix_post_g, p, i, *ffn_tail, ts=ts)
        else:
            x = _odd_mix(x, i, j, mix_pre_g, od_w_in, od_conv_w, od_conv_b, od_ln_g, od_ln_b, od_sconv_w,
                         od_w_out, mix_post_g, ts)
            x = _ffn_ple(x, p, i, *ffn_tail, ts=ts)
    return x
```

```python
import functools
import math

import jax
import jax.numpy as jnp
from jax import lax
from jax.experimental import pallas as pl
from jax.experimental.pallas import tpu as pltpu

EPS = 1e-6
LANES = 128
BF16_ROWS = 16
GROUPS = 4
BRANCH = GROUPS * LANES
CHUNK = 128
CONF_K = 31
CONF_HALO = 16
FFN_HALO = 8
V7X_VMEM_LIMIT = 56 * 1024 * 1024

F32 = jnp.float32
BF16 = jnp.bfloat16


def _rms(x, g):
    ms = jnp.mean(x * x, axis=-1, keepdims=True)
    return x * lax.rsqrt(ms + EPS) * g


def _layernorm(x, g, b):
    mu = jnp.mean(x, axis=-1, keepdims=True)
    xc = x - mu
    var = jnp.mean(xc * xc, axis=-1, keepdims=True)
    return xc * lax.rsqrt(var + EPS) * g + b


def _dot(a, b):
    return jnp.dot(a, b, preferred_element_type=F32)


def _resident(shape):
    nd = len(shape)
    return pl.BlockSpec(shape, lambda *_: (0,) * nd, pipeline_mode=pl.Buffered(1))


def _layer(arr, layer):
    nd = arr.ndim - 1
    return pl.BlockSpec((None,) + arr.shape[1:], lambda *_: (layer,) + (0,) * nd, pipeline_mode=pl.Buffered(1))


def _params(n_grid_axes):
    return pltpu.CompilerParams(
        dimension_semantics=("arbitrary",) * n_grid_axes,
        vmem_limit_bytes=V7X_VMEM_LIMIT,
    )


ROW_PITCH = 2


def _stage_slab(slab_ref, v, ts, halo):
    slab_ref[pl.ds(0, halo, stride=ROW_PITCH), :] = v[ts:ts + halo, :]
    slab_ref[pl.ds(ROW_PITCH * halo, ts, stride=ROW_PITCH), :] = v[0:ts, :]
    slab_ref[pl.ds(ROW_PITCH * (halo + ts), halo, stride=ROW_PITCH), :] = v[ts + halo:ts + 2 * halo, :]


def _slab_rows(slab_ref, first, n):
    return slab_ref[pl.ds(ROW_PITCH * first, n, stride=ROW_PITCH), :]


def _halo_specs(ts, halo, width, s):
    per = ts // halo
    last = s // halo - 1
    return [
        pl.BlockSpec((1, ts, width), lambda i, j: (i, j, 0)),
        pl.BlockSpec((1, halo, width), lambda i, j: (i, jnp.maximum(j * per - 1, 0), 0)),
        pl.BlockSpec((1, halo, width), lambda i, j: (i, jnp.minimum((j + 1) * per, last), 0)),
    ]


def _mask_halo(v, ts, halo):
    j = pl.program_id(1)
    row = lax.broadcasted_iota(jnp.int32, (ts + 2 * halo, 1), 0)
    outside = ((row >= ts) & (row < ts + halo) & (j == 0)) | ((row >= ts + halo) & (j == pl.num_programs(1) - 1))
    return jnp.where(outside, 0.0, v)


DFT_N1 = 256
DFT_N2 = 16
PERM_PITCH = 17


def _even_in_kernel(x_ref, g_ref, win_ref, wf_ref, lng_ref, lnb_ref, ws_ref, bs_ref,
                    z_ref, yb_ref, perm_ref, *, ts):
    h = _rms(x_ref[0], g_ref[...]).astype(BF16)
    pair = 2 * LANES
    n_blk = ts // DFT_N2

    def in_proj(q):
        return [_dot(h, win_ref[:, base + q * pair:base + (q + 1) * pair]) for base in (0, BRANCH, 2 * BRANCH)]

    def fourier_channel_stage(za, k):
        t = _dot(za.astype(BF16), wf_ref[k])
        for l in range(2):
            lanes = slice(l * LANES, (l + 1) * LANES)
            slab = perm_ref.at[2 * k + l]
            for j in range(n_blk):
                slab[pl.ds(ROW_PITCH * PERM_PITCH * j, DFT_N2, stride=ROW_PITCH), :] = (
                    t[j * DFT_N2:(j + 1) * DFT_N2, lanes])
            for n2 in range(DFT_N2):
                rows = slab[pl.ds(ROW_PITCH * n2, n_blk, stride=ROW_PITCH * PERM_PITCH), :]
                z_ref[0, n2, :, k * 2 * LANES + l * LANES:k * 2 * LANES + (l + 1) * LANES] = rows.astype(BF16)

    def spatial_gating(zu, zv, k):
        sl = slice(k * LANES, (k + 1) * LANES)
        vn = _layernorm(jax.nn.gelu(zv), lng_ref[:, sl], lnb_ref[:, sl]).astype(BF16)
        zu = jax.nn.gelu(zu)
        for c in range(ts // CHUNK):
            rows = slice(c * CHUNK, (c + 1) * CHUNK)
            sv = _dot(ws_ref[k], vn[rows, :]) + bs_ref[k]
            yb_ref[0, rows, sl] = (zu[rows, :] * sv).astype(BF16)

    z_next = in_proj(0)
    for q in range(GROUPS // 2):
        za, zu, zv = z_next
        if q + 1 < GROUPS // 2:
            z_next = in_proj(q + 1)
        for l in range(2):
            lanes = slice(l * LANES, (l + 1) * LANES)
            fourier_channel_stage(za[:, lanes], 2 * q + l)
            spatial_gating(zu[:, lanes], zv[:, lanes], 2 * q + l)


def _even_in(x, layer, j, pre_g, w_in, wf, ln_g, ln_b, ws, bs, ts):
    b, s, d = x.shape
    return pl.pallas_call(
        functools.partial(_even_in_kernel, ts=ts),
        grid=(b, s // ts),
        in_specs=[
            pl.BlockSpec((1, ts, d), lambda i, t: (i, t, 0)),
            _layer(pre_g, layer), _layer(w_in, j), _resident(wf.shape),
            _layer(ln_g, j), _layer(ln_b, j), _layer(ws, j), _resident(bs.shape),
        ],
        out_specs=[
            pl.BlockSpec((1, DFT_N2, ts // DFT_N2, 2 * BRANCH), lambda i, t: (i, 0, t, 0)),
            pl.BlockSpec((1, ts, BRANCH), lambda i, t: (i, t, 0)),
        ],
        out_shape=[
            jax.ShapeDtypeStruct((b, DFT_N2, s // DFT_N2, 2 * BRANCH), BF16),
            jax.ShapeDtypeStruct((b, s, BRANCH), BF16),
        ],
        scratch_shapes=[
            pltpu.VMEM((2 * GROUPS, ROW_PITCH * PERM_PITCH * (ts // DFT_N2), LANES), F32),
        ],
        compiler_params=_params(2),
        name="even_in",
    )(x, pre_g, w_in, wf, ln_g, ln_b, ws, bs)


DFT_ROWS = 16


def _dft16_real(br, bi):
    u = [[None] * 4 for _ in range(4)]
    for nb in range(4):
        x0r, x1r, x2r, x3r = br[nb], br[4 + nb], br[8 + nb], br[12 + nb]
        x0i, x1i, x2i, x3i = bi[nb], bi[4 + nb], bi[8 + nb], bi[12 + nb]
        t0r, t0i = x0r + x2r, x0i + x2i
        t1r, t1i = x0r - x2r, x0i - x2i
        t2r, t2i = x1r + x3r, x1i + x3i
        t3r, t3i = x1r - x3r, x1i - x3i
        u[0][nb] = (t0r + t2r, t0i + t2i)
        u[2][nb] = (t0r - t2r, t0i - t2i)
        u[1][nb] = (t1r + t3i, t1i - t3r)
        u[3][nb] = (t1r - t3i, t1i + t3r)
    out = [None] * 16
    for ka in range(4):
        vr, vi = [], []
        for nb in range(4):
            a, b = u[ka][nb]
            e = (nb * ka) % 16
            wr, wi = math.cos(-2.0 * math.pi * e / 16), math.sin(-2.0 * math.pi * e / 16)
            if e == 0:
                vr.append(a)
                vi.append(b)
            elif e == 4:
                vr.append(b)
                vi.append(-a)
            else:
                vr.append(a * wr - b * wi)
                vi.append(a * wi + b * wr if nb in (1, 3) else None)
        s0, s1, s2, s3 = vr[0] + vr[2], vr[0] - vr[2], vr[1] + vr[3], vi[1] - vi[3]
        out[ka] = s0 + s2
        out[ka + 4] = s1 + s3
        out[ka + 8] = s0 - s2
        out[ka + 12] = s1 - s3
    return out


def _seq_dft_kernel(tab_ref, z_ref, o_ref, bre_ref, bim_ref):
    def matmul_stage(g):
        for n2 in range(DFT_N2):
            t = _dot(tab_ref[n2], z_ref[0, n2, :, g * 2 * LANES:(g + 1) * 2 * LANES])
            bre_ref[g % 2, n2] = t[:DFT_N1, :LANES] - t[DFT_N1:, LANES:]
            bim_ref[g % 2, n2] = t[:DFT_N1, LANES:] + t[DFT_N1:, :LANES]

    def slab_stage(g):
        for r in range(DFT_N1 // DFT_ROWS):
            rows = slice(r * DFT_ROWS, (r + 1) * DFT_ROWS)
            y = _dft16_real([bre_ref[g % 2, n2, rows, :] for n2 in range(DFT_N2)],
                            [bim_ref[g % 2, n2, rows, :] for n2 in range(DFT_N2)])
            for k2 in range(DFT_N2):
                o_ref[0, k2 * DFT_N1 + r * DFT_ROWS:k2 * DFT_N1 + (r + 1) * DFT_ROWS,
                      g * LANES:(g + 1) * LANES] = y[k2].astype(BF16)

    matmul_stage(0)
    for g in range(GROUPS):
        if g + 1 < GROUPS:
            matmul_stage(g + 1)
        slab_stage(g)


def _seq_dft(tab, z):
    b, n2, n1, width = z.shape
    assert (n2, n1) == (DFT_N2, DFT_N1), "the factored sequence DFT is written for S = 4096"
    return pl.pallas_call(
        _seq_dft_kernel,
        grid=(b,),
        in_specs=[
            _resident(tab.shape),
            pl.BlockSpec((1, n2, n1, width), lambda i: (i, 0, 0, 0)),
        ],
        out_specs=pl.BlockSpec((1, n1 * n2, BRANCH), lambda i: (i, 0, 0)),
        out_shape=jax.ShapeDtypeStruct((b, n1 * n2, BRANCH), BF16),
        scratch_shapes=[
            pltpu.VMEM((2, n2, n1, LANES), F32),
            pltpu.VMEM((2, n2, n1, LANES), F32),
        ],
        compiler_params=_params(1),
        name="seq_dft",
    )(tab, z)


CONV_ROWS = 64


def _odd_mix_kernel(x_ref, prev_ref, next_ref, pre_g_ref, win_ref, cw_ref, cb_ref, lng_ref, lnb_ref,
                    sw_ref, wout_ref, post_g_ref, o_ref, gslab_ref, mslab_ref, *, ts):
    halo = CONF_HALO
    pair = 2 * LANES
    x_all = jnp.concatenate([x_ref[0], prev_ref[0], next_ref[0]], axis=0)
    h = _mask_halo(_rms(x_all, pre_g_ref[...]), ts, halo).astype(BF16)

    def proj(first_col, width):
        return _dot(h, win_ref[:, first_col:first_col + width])

    def glu_conv_norm(a, gate, first_group):
        glu = a * jax.nn.sigmoid(gate)
        outs = []
        for l in range(2):
            k = first_group + l
            sl = slice(k * LANES, (k + 1) * LANES)
            slab = gslab_ref.at[k]
            _stage_slab(slab, glu[:, l * LANES:(l + 1) * LANES], ts, halo)
            first = halo - CONF_K // 2
            blocks = []
            for r in range(ts // CONV_ROWS):
                r0 = r * CONV_ROWS + first
                acc = _slab_rows(slab, r0, CONV_ROWS) * cw_ref[0:1, sl]
                for t in range(1, CONF_K):
                    acc = acc + _slab_rows(slab, r0 + t, CONV_ROWS) * cw_ref[t:t + 1, sl]
                blocks.append(acc)
            c = jnp.concatenate(blocks, axis=0) + cb_ref[:, sl]
            outs.append(jax.nn.silu(_layernorm(c, lng_ref[:, sl], lnb_ref[:, sl])))
        return jnp.concatenate(outs, axis=-1).astype(BF16)

    a0, g0 = proj(0, pair), proj(BRANCH, pair)
    a1, g1 = proj(pair, pair), proj(BRANCH + pair, pair)
    yc0 = glu_conv_norm(a0, g0, 0)
    bg = proj(2 * BRANCH, BRANCH)[:ts]
    cg = proj(3 * BRANCH, BRANCH)
    xin = proj(4 * BRANCH, BRANCH)
    yc1 = glu_conv_norm(a1, g1, 2)
    m = cg * xin
    yd = []
    for k in range(GROUPS):
        sl = slice(k * LANES, (k + 1) * LANES)
        slab = mslab_ref.at[k]
        _stage_slab(slab, m[:, sl], ts, halo)
        conv = (_slab_rows(slab, halo - 1, ts) * sw_ref[0:1, sl] + m[0:ts, sl] * sw_ref[1:2, sl]
                + _slab_rows(slab, halo + 1, ts) * sw_ref[2:3, sl])
        yd.append(bg[:, sl] * conv)
    ydc = jnp.concatenate(yd, axis=-1).astype(BF16)
    blk = ts // TAIL_BLOCKS
    for i in range(TAIL_BLOCKS):
        rows = slice(i * blk, (i + 1) * blk)
        y = (_dot(yc0[rows], wout_ref[0:pair, :]) + _dot(yc1[rows], wout_ref[pair:BRANCH, :])
             + _dot(ydc[rows], wout_ref[BRANCH:, :]))
        o_ref[0, rows, :] = x_ref[0, rows, :] + _rms(y, post_g_ref[...])


def _odd_mix(x, layer, j, pre_g, w_in, conv_w, conv_b, ln_g, ln_b, sconv_w, w_out, post_g, ts):
    b, s, d = x.shape
    halo = CONF_HALO
    return pl.pallas_call(
        functools.partial(_odd_mix_kernel, ts=ts),
        grid=(b, s // ts),
        in_specs=_halo_specs(ts, halo, d, s) + [
            _layer(pre_g, layer), _layer(w_in, j), _layer(conv_w, j), _layer(conv_b, j), _layer(ln_g, j),
            _layer(ln_b, j), _layer(sconv_w, j), _layer(w_out, j), _layer(post_g, layer)],
        out_specs=pl.BlockSpec((1, ts, d), lambda i, t: (i, t, 0)),
        out_shape=jax.ShapeDtypeStruct(x.shape, F32),
        scratch_shapes=[
            pltpu.VMEM((GROUPS, ROW_PITCH * (ts + 2 * halo), LANES), F32),
            pltpu.VMEM((GROUPS, ROW_PITCH * (ts + 2 * halo), LANES), F32),
        ],
        compiler_params=_params(2),
        name="odd_mix",
    )(x, x, x, pre_g, w_in, conv_w, conv_b, ln_g, ln_b, sconv_w, w_out, post_g)


FF_CHUNK = 256
TAIL_BLOCKS = 4
MIX_BLOCKS = 3


def _ffn_ple_tail(x1_all, p_ref, pre_g_ref, wup_ref, cw_ref, cb_ref, wdown_ref, post_g_ref, gate_g_ref,
                  wg_ref, bg_ref, wp_ref, o_ref, ext_ref, act_ref, ts):
    halo = FFN_HALO
    d_ff = wdown_ref.shape[0]
    n_slab = FF_CHUNK // LANES
    n_chunks = d_ff // FF_CHUNK
    h = _mask_halo(_rms(x1_all, pre_g_ref[...]), ts, halo).astype(BF16)

    def up_proj(c):
        return [_dot(h, wup_ref[:, base + c * FF_CHUNK:base + (c + 1) * FF_CHUNK]) for base in (0, d_ff)]

    z_next = up_proj(0)
    for c in range(n_chunks):
        z_cur = z_next
        if c + 1 < n_chunks:
            z_next = up_proj(c + 1)
        halves = []
        for half, base in enumerate((0, d_ff)):
            cols = slice(base + c * FF_CHUNK, base + (c + 1) * FF_CHUNK)
            zc = z_cur[half]
            w = cw_ref[:, cols]
            bias = cb_ref[:, cols]
            outs = []
            for l in range(n_slab):
                lanes = slice(l * LANES, (l + 1) * LANES)
                slab = ext_ref.at[c % 2, half * n_slab + l]
                _stage_slab(slab, zc[:, lanes], ts, halo)
                outs.append(_slab_rows(slab, halo - 1, ts) * w[0:1, lanes]
                            + zc[0:ts, lanes] * w[1:2, lanes]
                            + _slab_rows(slab, halo + 1, ts) * w[2:3, lanes]
                            + bias[:, lanes])
            halves.append(jnp.concatenate(outs, axis=-1))
        act_ref[:, c * FF_CHUNK:(c + 1) * FF_CHUNK] = (jax.nn.gelu(halves[0]) * halves[1]).astype(BF16)
    blk = ts // TAIL_BLOCKS
    fs = [_dot(act_ref[i * blk:(i + 1) * blk, :], wdown_ref[...]) for i in range(TAIL_BLOCKS)]
    emb = _dot(p_ref[0].astype(BF16), wp_ref[...])
    for i, f in enumerate(fs):
        rows = slice(i * blk, (i + 1) * blk)
        x2 = x1_all[rows] + _rms(f, post_g_ref[...])
        hg = _rms(x2, gate_g_ref[...]).astype(BF16)
        gate = jax.nn.sigmoid(_dot(hg, wg_ref[...]) + bg_ref[...])
        o_ref[0, rows, :] = x2 + gate * emb[rows]


def _ffn_ple_kernel(x_ref, prev_ref, next_ref, p_ref, *rest, ts):
    x1_all = jnp.concatenate([x_ref[0], prev_ref[0], next_ref[0]], axis=0)
    _ffn_ple_tail(x1_all, p_ref, *rest, ts)


def _even_tail_kernel(x_ref, prev_ref, next_ref, ya_ref, ya_prev_ref, ya_next_ref, yb_ref, yb_prev_ref,
                      yb_next_ref, wout_ref, mix_g_ref, p_ref, *rest, ts):
    lo = BF16_ROWS - FFN_HALO

    def with_halo(main_ref, before_ref, after_ref):
        halo = jnp.concatenate([before_ref[0].astype(F32)[lo:], after_ref[0].astype(F32)[:FFN_HALO]], axis=0)
        return jnp.concatenate([main_ref[0], halo.astype(BF16)], axis=0)

    x_all = jnp.concatenate([x_ref[0], prev_ref[0], next_ref[0]], axis=0)
    ya = with_halo(ya_ref, ya_prev_ref, ya_next_ref)
    yb = with_halo(yb_ref, yb_prev_ref, yb_next_ref)
    n_groups = (ts + 2 * FFN_HALO) // BF16_ROWS
    cuts = [BF16_ROWS * (n_groups * i // MIX_BLOCKS) for i in range(MIX_BLOCKS + 1)]
    parts = []
    for lo, hi in zip(cuts[:-1], cuts[1:]):
        y = _dot(ya[lo:hi], wout_ref[:BRANCH, :]) + _dot(yb[lo:hi], wout_ref[BRANCH:, :])
        parts.append(x_all[lo:hi] + _rms(y, mix_g_ref[...]))
    _ffn_ple_tail(jnp.concatenate(parts, axis=0), p_ref, *rest, ts)


def _ffn_call(kernel_fn, name, front_specs, front_args, x, p, layer, pre_g, w_up, conv_w, conv_b, w_down,
              post_g, gate_g, w_g, b_g, w_p, ts):
    b, s, d = x.shape
    halo = FFN_HALO
    d_ff = w_down.shape[1]
    tail = (pre_g, w_up, conv_w, conv_b, w_down, post_g, gate_g, w_g, b_g, w_p)
    return pl.pallas_call(
        functools.partial(kernel_fn, ts=ts),
        grid=(b, s // ts),
        in_specs=_halo_specs(ts, halo, d, s) + front_specs
        + [pl.BlockSpec((None, 1, ts, p.shape[-1]), lambda i, t: (layer, i, t, 0))]
        + [_layer(c, layer) for c in tail],
        out_specs=pl.BlockSpec((1, ts, d), lambda i, t: (i, t, 0)),
        out_shape=jax.ShapeDtypeStruct(x.shape, F32),
        scratch_shapes=[
            pltpu.VMEM((2, 2 * FF_CHUNK // LANES, ROW_PITCH * (ts + 2 * halo), LANES), F32),
            pltpu.VMEM((ts, d_ff), BF16),
        ],
        compiler_params=_params(2),
        name=name,
    )(x, x, x, *front_args, p, *tail)


def _ffn_ple(x, p, layer, *tail, ts):
    return _ffn_call(_ffn_ple_kernel, "ffn_ple", [], [], x, p, layer, *tail, ts)


def _even_tail(x, ya, yb, j, w_out, mix_g, p, layer, *tail, ts):
    s = x.shape[1]
    front_specs = (_halo_specs(ts, BF16_ROWS, BRANCH, s) + _halo_specs(ts, BF16_ROWS, BRANCH, s)
                   + [_layer(w_out, j), _layer(mix_g, layer)])
    return _ffn_call(_even_tail_kernel, "even_tail", front_specs, [ya, ya, ya, yb, yb, yb, w_out, mix_g],
                     x, p, layer, *tail, ts)


def _channel_dft_kernel(cs_ref, wf_ref, o_ref):
    for g in range(GROUPS):
        for l in range(2):
            o_ref[g, :, l * LANES:(l + 1) * LANES] = jnp.dot(
                cs_ref[l], wf_ref[g], precision=lax.Precision.HIGHEST, preferred_element_type=F32).astype(BF16)


def _channel_dft_weights(w_f, s):
    n = lax.broadcasted_iota(jnp.int32, (LANES, LANES), 0) * lax.broadcasted_iota(jnp.int32, (LANES, LANES), 1)
    ang = (n % LANES).astype(F32) * (2.0 * math.pi / LANES)
    scale = 1.0 / math.sqrt(float(s) * LANES)
    cs = jnp.stack([jnp.cos(ang) * scale, -jnp.sin(ang) * scale])
    return pl.pallas_call(
        _channel_dft_kernel,
        out_shape=jax.ShapeDtypeStruct((GROUPS, LANES, 2 * LANES), BF16),
        name="channel_dft_weights",
    )(cs, w_f)


def _seq_dft_table():
    shape = (DFT_N2, DFT_N1, DFT_N1)
    n2 = lax.broadcasted_iota(jnp.int32, shape, 0)
    k1 = lax.broadcasted_iota(jnp.int32, shape, 1)
    n1 = lax.broadcasted_iota(jnp.int32, shape, 2)
    s = DFT_N1 * DFT_N2
    ang = (((n2 + DFT_N2 * n1) * k1) % s).astype(F32) * (-2.0 * math.pi / s)
    return jnp.concatenate([jnp.cos(ang), jnp.sin(ang)], axis=1).astype(BF16)


def _rows(v):
    return v[:, None, :]


def kernel(x, p, mix_pre_g, mix_post_g, ffn_pre_g, ffn_post_g, ev_w_in, ev_w_fourier, ev_v_ln_g, ev_v_ln_b, ev_w_spatial, ev_b_spatial, ev_w_out, od_w_in, od_conv_w, od_conv_b, od_ln_g, od_ln_b, od_sconv_w, od_w_out, ffn_w_up, ffn_conv_w, ffn_conv_b, ffn_w_down, ple_w_p, ple_gate_g, ple_w_g, ple_b_g):
    depth = mix_pre_g.shape[0]
    s = x.shape[1]
    ts = min(512, s)
    tab = _seq_dft_table()
    mix_pre_g, mix_post_g = _rows(mix_pre_g), _rows(mix_post_g)
    ffn_tail = (_rows(ffn_pre_g), ffn_w_up.astype(BF16), ffn_conv_w, _rows(ffn_conv_b), ffn_w_down.astype(BF16),
                _rows(ffn_post_g), _rows(ple_gate_g), ple_w_g.astype(BF16), _rows(ple_b_g), ple_w_p.astype(BF16))
    ev_w_in, ev_w_spatial, ev_w_out = ev_w_in.astype(BF16), ev_w_spatial.astype(BF16), ev_w_out.astype(BF16)
    ev_v_ln_g, ev_v_ln_b = _rows(ev_v_ln_g), _rows(ev_v_ln_b)
    od_w_in, od_w_out = od_w_in.astype(BF16), od_w_out.astype(BF16)
    od_conv_b, od_ln_g, od_ln_b = _rows(od_conv_b), _rows(od_ln_g), _rows(od_ln_b)
    for i in range(depth):
        j = i // 2
        if i % 2 == 0:
            wf = _channel_dft_weights(ev_w_fourier[j], s)
            bs = jnp.broadcast_to(ev_b_spatial[j][:, :, None], (GROUPS, CHUNK, LANES))
            z, yb = _even_in(x, i, j, mix_pre_g, ev_w_in, wf, ev_v_ln_g, ev_v_ln_b, ev_w_spatial, bs, ts)
            ya = _seq_dft(tab, z)
            x = _even_tail(x, ya, yb, j, ev_w_out, mix_post_g, p, i, *ffn_tail, ts=ts)
        else:
            x = _odd_mix(x, i, j, mix_pre_g, od_w_in, od_conv_w, od_conv_b, od_ln_g, od_ln_b, od_sconv_w,
                         od_w_out, mix_post_g, ts)
            x = _ffn_ple(x, p, i, *ffn_tail, ts=ts)
    return x
```

```python
import functools
import math

import jax
import jax.numpy as jnp
from jax import lax
from jax.experimental import pallas as pl
from jax.experimental.pallas import tpu as pltpu

EPS = 1e-6
LANES = 128
BF16_ROWS = 16
GROUPS = 4
BRANCH = GROUPS * LANES
CHUNK = 128
CONF_K = 31
CONF_HALO = 16
FFN_HALO = 8
V7X_VMEM_LIMIT = 56 * 1024 * 1024

F32 = jnp.float32
BF16 = jnp.bfloat16


def _rms(x, g):
    ms = jnp.mean(x * x, axis=-1, keepdims=True)
    return x * lax.rsqrt(ms + EPS) * g


def _layernorm(x, g, b):
    mu = jnp.mean(x, axis=-1, keepdims=True)
    xc = x - mu
    var = jnp.mean(xc * xc, axis=-1, keepdims=True)
    return xc * lax.rsqrt(var + EPS) * g + b


def _dot(a, b):
    return jnp.dot(a, b, preferred_element_type=F32)


def _resident(shape):
    nd = len(shape)
    return pl.BlockSpec(shape, lambda *_: (0,) * nd, pipeline_mode=pl.Buffered(1))


def _layer(arr, layer):
    nd = arr.ndim - 1
    return pl.BlockSpec((None,) + arr.shape[1:], lambda *_: (layer,) + (0,) * nd, pipeline_mode=pl.Buffered(1))


def _params(n_grid_axes):
    return pltpu.CompilerParams(
        dimension_semantics=("arbitrary",) * n_grid_axes,
        vmem_limit_bytes=V7X_VMEM_LIMIT,
    )


ROW_PITCH = 2


def _stage_slab(slab_ref, v, ts, halo):
    slab_ref[pl.ds(0, halo, stride=ROW_PITCH), :] = v[ts:ts + halo, :]
    slab_ref[pl.ds(ROW_PITCH * halo, ts, stride=ROW_PITCH), :] = v[0:ts, :]
    slab_ref[pl.ds(ROW_PITCH * (halo + ts), halo, stride=ROW_PITCH), :] = v[ts + halo:ts + 2 * halo, :]


def _slab_rows(slab_ref, first, n):
    return slab_ref[pl.ds(ROW_PITCH * first, n, stride=ROW_PITCH), :]


def _halo_specs(ts, halo, width, s):
    per = ts // halo
    last = s // halo - 1
    return [
        pl.BlockSpec((1, ts, width), lambda i, j: (i, j, 0)),
        pl.BlockSpec((1, halo, width), lambda i, j: (i, jnp.maximum(j * per - 1, 0), 0)),
        pl.BlockSpec((1, halo, width), lambda i, j: (i, jnp.minimum((j + 1) * per, last), 0)),
    ]


def _mask_halo(v, ts, halo):
    j = pl.program_id(1)
    row = lax.broadcasted_iota(jnp.int32, (ts + 2 * halo, 1), 0)
    outside = ((row >= ts) & (row < ts + halo) & (j == 0)) | ((row >= ts + halo) & (j == pl.num_programs(1) - 1))
    return jnp.where(outside, 0.0, v)


DFT_N1 = 256
DFT_N2 = 16
PERM_PITCH = 17


def _even_in_kernel(x_ref, g_ref, win_ref, wf_ref, lng_ref, lnb_ref, ws_ref, bs_ref,
                    z_ref, yb_ref, perm_ref, *, ts):
    h = _rms(x_ref[0], g_ref[...]).astype(BF16)
    pair = 2 * LANES
    n_blk = ts // DFT_N2

    def in_proj(q):
        return [_dot(h, win_ref[:, base + q * pair:base + (q + 1) * pair]) for base in (0, BRANCH, 2 * BRANCH)]

    def fourier_channel_stage(za, k):
        t = _dot(za.astype(BF16), wf_ref[k])
        for l in range(2):
            lanes = slice(l * LANES, (l + 1) * LANES)
            slab = perm_ref.at[2 * k + l]
            for j in range(n_blk):
                slab[pl.ds(ROW_PITCH * PERM_PITCH * j, DFT_N2, stride=ROW_PITCH), :] = (
                    t[j * DFT_N2:(j + 1) * DFT_N2, lanes])
            for n2 in range(DFT_N2):
                rows = slab[pl.ds(ROW_PITCH * n2, n_blk, stride=ROW_PITCH * PERM_PITCH), :]
                z_ref[0, n2, :, k * 2 * LANES + l * LANES:k * 2 * LANES + (l + 1) * LANES] = rows.astype(BF16)

    def spatial_gating(zu, zv, k):
        sl = slice(k * LANES, (k + 1) * LANES)
        vn = _layernorm(jax.nn.gelu(zv), lng_ref[:, sl], lnb_ref[:, sl]).astype(BF16)
        zu = jax.nn.gelu(zu)
        for c in range(ts // CHUNK):
            rows = slice(c * CHUNK, (c + 1) * CHUNK)
            sv = _dot(ws_ref[k], vn[rows, :]) + bs_ref[k]
            yb_ref[0, rows, sl] = (zu[rows, :] * sv).astype(BF16)

    z_next = in_proj(0)
    for q in range(GROUPS // 2):
        za, zu, zv = z_next
        if q + 1 < GROUPS // 2:
            z_next = in_proj(q + 1)
        for l in range(2):
            lanes = slice(l * LANES, (l + 1) * LANES)
            fourier_channel_stage(za[:, lanes], 2 * q + l)
            spatial_gating(zu[:, lanes], zv[:, lanes], 2 * q + l)


def _even_in(x, layer, j, pre_g, w_in, wf, ln_g, ln_b, ws, bs, ts):
    b, s, d = x.shape
    return pl.pallas_call(
        functools.partial(_even_in_kernel, ts=ts),
        grid=(b, s // ts),
        in_specs=[
            pl.BlockSpec((1, ts, d), lambda i, t: (i, t, 0)),
            _layer(pre_g, layer), _layer(w_in, j), _resident(wf.shape),
            _layer(ln_g, j), _layer(ln_b, j), _layer(ws, j), _resident(bs.shape),
        ],
        out_specs=[
            pl.BlockSpec((1, DFT_N2, ts // DFT_N2, 2 * BRANCH), lambda i, t: (i, 0, t, 0)),
            pl.BlockSpec((1, ts, BRANCH), lambda i, t: (i, t, 0)),
        ],
        out_shape=[
            jax.ShapeDtypeStruct((b, DFT_N2, s // DFT_N2, 2 * BRANCH), BF16),
            jax.ShapeDtypeStruct((b, s, BRANCH), BF16),
        ],
        scratch_shapes=[
            pltpu.VMEM((2 * GROUPS, ROW_PITCH * PERM_PITCH * (ts // DFT_N2), LANES), F32),
        ],
        compiler_params=_params(2),
        name="even_in",
    )(x, pre_g, w_in, wf, ln_g, ln_b, ws, bs)


DFT_ROWS = 16


def _dft16_real(br, bi):
    u = [[None] * 4 for _ in range(4)]
    for nb in range(4):
        x0r, x1r, x2r, x3r = br[nb], br[4 + nb], br[8 + nb], br[12 + nb]
        x0i, x1i, x2i, x3i = bi[nb], bi[4 + nb], bi[8 + nb], bi[12 + nb]
        t0r, t0i = x0r + x2r, x0i + x2i
        t1r, t1i = x0r - x2r, x0i - x2i
        t2r, t2i = x1r + x3r, x1i + x3i
        t3r, t3i = x1r - x3r, x1i - x3i
        u[0][nb] = (t0r + t2r, t0i + t2i)
        u[2][nb] = (t0r - t2r, t0i - t2i)
        u[1][nb] = (t1r + t3i, t1i - t3r)
        u[3][nb] = (t1r - t3i, t1i + t3r)
    out = [None] * 16
    for ka in range(4):
        vr, vi = [], []
        for nb in range(4):
            a, b = u[ka][nb]
            e = (nb * ka) % 16
            wr, wi = math.cos(-2.0 * math.pi * e / 16), math.sin(-2.0 * math.pi * e / 16)
            if e == 0:
                vr.append(a)
                vi.append(b)
            elif e == 4:
                vr.append(b)
                vi.append(-a)
            else:
                vr.append(a * wr - b * wi)
                vi.append(a * wi + b * wr if nb in (1, 3) else None)
        s0, s1, s2, s3 = vr[0] + vr[2], vr[0] - vr[2], vr[1] + vr[3], vi[1] - vi[3]
        out[ka] = s0 + s2
        out[ka + 4] = s1 + s3
        out[ka + 8] = s0 - s2
        out[ka + 12] = s1 - s3
    return out


def _seq_dft_kernel(tab_ref, z_ref, o_ref, bre_ref, bim_ref):
    def matmul_stage(g):
        for n2 in range(DFT_N2):
            t = _dot(tab_ref[n2], z_ref[0, n2, :, g * 2 * LANES:(g + 1) * 2 * LANES])
            bre_ref[g % 2, n2] = t[:DFT_N1, :LANES] - t[DFT_N1:, LANES:]
            bim_ref[g % 2, n2] = t[:DFT_N1, LANES:] + t[DFT_N1:, :LANES]

    def slab_stage(g):
        for r in range(DFT_N1 // DFT_ROWS):
            rows = slice(r * DFT_ROWS, (r + 1) * DFT_ROWS)
            y = _dft16_real([bre_ref[g % 2, n2, rows, :] for n2 in range(DFT_N2)],
                            [bim_ref[g % 2, n2, rows, :] for n2 in range(DFT_N2)])
            for k2 in range(DFT_N2):
                o_ref[0, k2 * DFT_N1 + r * DFT_ROWS:k2 * DFT_N1 + (r + 1) * DFT_ROWS,
                      g * LANES:(g + 1) * LANES] = y[k2].astype(BF16)

    matmul_stage(0)
    for g in range(GROUPS):
        if g + 1 < GROUPS:
            matmul_stage(g + 1)
        slab_stage(g)


def _seq_dft(tab, z):
    b, n2, n1, width = z.shape
    assert (n2, n1) == (DFT_N2, DFT_N1), "the factored sequence DFT is written for S = 4096"
    return pl.pallas_call(
        _seq_dft_kernel,
        grid=(b,),
        in_specs=[
            _resident(tab.shape),
            pl.BlockSpec((1, n2, n1, width), lambda i: (i, 0, 0, 0)),
        ],
        out_specs=pl.BlockSpec((1, n1 * n2, BRANCH), lambda i: (i, 0, 0)),
        out_shape=jax.ShapeDtypeStruct((b, n1 * n2, BRANCH), BF16),
        scratch_shapes=[
            pltpu.VMEM((2, n2, n1, LANES), F32),
            pltpu.VMEM((2, n2, n1, LANES), F32),
        ],
        compiler_params=_params(1),
        name="seq_dft",
    )(tab, z)


CONV_ROWS = 64


def _odd_mix_kernel(x_ref, prev_ref, next_ref, pre_g_ref, win_ref, cw_ref, cb_ref, lng_ref, lnb_ref,
                    sw_ref, wout_ref, post_g_ref, o_ref, gslab_ref, mslab_ref, *, ts):
    halo = CONF_HALO
    pair = 2 * LANES
    x_all = jnp.concatenate([x_ref[0], prev_ref[0], next_ref[0]], axis=0)
    h = _mask_halo(_rms(x_all, pre_g_ref[...]), ts, halo).astype(BF16)

    def proj(first_col, width):
        return _dot(h, win_ref[:, first_col:first_col + width])

    def glu_conv_norm(a, gate, first_group):
        glu = a * jax.nn.sigmoid(gate)
        outs = []
        for l in range(2):
            k = first_group + l
            sl = slice(k * LANES, (k + 1) * LANES)
            slab = gslab_ref.at[k]
            _stage_slab(slab, glu[:, l * LANES:(l + 1) * LANES], ts, halo)
            first = halo - CONF_K // 2
            blocks = []
            for r in range(ts // CONV_ROWS):
                r0 = r * CONV_ROWS + first
                acc = _slab_rows(slab, r0, CONV_ROWS) * cw_ref[0:1, sl]
                for t in range(1, CONF_K):
                    acc = acc + _slab_rows(slab, r0 + t, CONV_ROWS) * cw_ref[t:t + 1, sl]
                blocks.append(acc)
            c = jnp.concatenate(blocks, axis=0) + cb_ref[:, sl]
            outs.append(jax.nn.silu(_layernorm(c, lng_ref[:, sl], lnb_ref[:, sl])))
        return jnp.concatenate(outs, axis=-1).astype(BF16)

    a0, g0 = proj(0, pair), proj(BRANCH, pair)
    a1, g1 = proj(pair, pair), proj(BRANCH + pair, pair)
    yc0 = glu_conv_norm(a0, g0, 0)
    bg = proj(2 * BRANCH, BRANCH)[:ts]
    cg = proj(3 * BRANCH, BRANCH)
    xin = proj(4 * BRANCH, BRANCH)
    yc1 = glu_conv_norm(a1, g1, 2)
    y = _dot(yc0, wout_ref[0:pair, :]) + _dot(yc1, wout_ref[pair:BRANCH, :])
    m = cg * xin
    yd = []
    for k in range(GROUPS):
        sl = slice(k * LANES, (k + 1) * LANES)
        slab = mslab_ref.at[k]
        _stage_slab(slab, m[:, sl], ts, halo)
        conv = (_slab_rows(slab, halo - 1, ts) * sw_ref[0:1, sl] + m[0:ts, sl] * sw_ref[1:2, sl]
                + _slab_rows(slab, halo + 1, ts) * sw_ref[2:3, sl])
        yd.append(bg[:, sl] * conv)
    y = y + _dot(jnp.concatenate(yd, axis=-1).astype(BF16), wout_ref[BRANCH:, :])
    o_ref[0] = x_ref[0] + _rms(y, post_g_ref[...])


def _odd_mix(x, layer, j, pre_g, w_in, conv_w, conv_b, ln_g, ln_b, sconv_w, w_out, post_g, ts):
    b, s, d = x.shape
    halo = CONF_HALO
    return pl.pallas_call(
        functools.partial(_odd_mix_kernel, ts=ts),
        grid=(b, s // ts),
        in_specs=_halo_specs(ts, halo, d, s) + [
            _layer(pre_g, layer), _layer(w_in, j), _layer(conv_w, j), _layer(conv_b, j), _layer(ln_g, j),
            _layer(ln_b, j), _layer(sconv_w, j), _layer(w_out, j), _layer(post_g, layer)],
        out_specs=pl.BlockSpec((1, ts, d), lambda i, t: (i, t, 0)),
        out_shape=jax.ShapeDtypeStruct(x.shape, F32),
        scratch_shapes=[
            pltpu.VMEM((GROUPS, ROW_PITCH * (ts + 2 * halo), LANES), F32),
            pltpu.VMEM((GROUPS, ROW_PITCH * (ts + 2 * halo), LANES), F32),
        ],
        compiler_params=_params(2),
        name="odd_mix",
    )(x, x, x, pre_g, w_in, conv_w, conv_b, ln_g, ln_b, sconv_w, w_out, post_g)


FF_CHUNK = 256
TAIL_BLOCKS = 4
MIX_BLOCKS = 3


def _ffn_ple_tail(x1_all, p_ref, pre_g_ref, wup_ref, cw_ref, cb_ref, wdown_ref, post_g_ref, gate_g_ref,
                  wg_ref, bg_ref, wp_ref, o_ref, ext_ref, act_ref, ts):
    halo = FFN_HALO
    d_ff = wdown_ref.shape[0]
    n_slab = FF_CHUNK // LANES
    n_chunks = d_ff // FF_CHUNK
    h = _mask_halo(_rms(x1_all, pre_g_ref[...]), ts, halo).astype(BF16)

    def up_proj(c):
        return [_dot(h, wup_ref[:, base + c * FF_CHUNK:base + (c + 1) * FF_CHUNK]) for base in (0, d_ff)]

    z_next = up_proj(0)
    for c in range(n_chunks):
        z_cur = z_next
        if c + 1 < n_chunks:
            z_next = up_proj(c + 1)
        halves = []
        for half, base in enumerate((0, d_ff)):
            cols = slice(base + c * FF_CHUNK, base + (c + 1) * FF_CHUNK)
            zc = z_cur[half]
            w = cw_ref[:, cols]
            bias = cb_ref[:, cols]
            outs = []
            for l in range(n_slab):
                lanes = slice(l * LANES, (l + 1) * LANES)
                slab = ext_ref.at[c % 2, half * n_slab + l]
                _stage_slab(slab, zc[:, lanes], ts, halo)
                outs.append(_slab_rows(slab, halo - 1, ts) * w[0:1, lanes]
                            + zc[0:ts, lanes] * w[1:2, lanes]
                            + _slab_rows(slab, halo + 1, ts) * w[2:3, lanes]
                            + bias[:, lanes])
            halves.append(jnp.concatenate(outs, axis=-1))
        act_ref[:, c * FF_CHUNK:(c + 1) * FF_CHUNK] = (jax.nn.gelu(halves[0]) * halves[1]).astype(BF16)
    blk = ts // TAIL_BLOCKS
    fs = [_dot(act_ref[i * blk:(i + 1) * blk, :], wdown_ref[...]) for i in range(TAIL_BLOCKS)]
    emb = _dot(p_ref[0].astype(BF16), wp_ref[...])
    for i, f in enumerate(fs):
        rows = slice(i * blk, (i + 1) * blk)
        x2 = x1_all[rows] + _rms(f, post_g_ref[...])
        hg = _rms(x2, gate_g_ref[...]).astype(BF16)
        gate = jax.nn.sigmoid(_dot(hg, wg_ref[...]) + bg_ref[...])
        o_ref[0, rows, :] = x2 + gate * emb[rows]


def _ffn_ple_kernel(x_ref, prev_ref, next_ref, p_ref, *rest, ts):
    x1_all = jnp.concatenate([x_ref[0], prev_ref[0], next_ref[0]], axis=0)
    _ffn_ple_tail(x1_all, p_ref, *rest, ts)


def _even_tail_kernel(x_ref, prev_ref, next_ref, ya_ref, ya_prev_ref, ya_next_ref, yb_ref, yb_prev_ref,
                      yb_next_ref, wout_ref, mix_g_ref, p_ref, *rest, ts):
    lo = BF16_ROWS - FFN_HALO

    def with_halo(main_ref, before_ref, after_ref):
        halo = jnp.concatenate([before_ref[0].astype(F32)[lo:], after_ref[0].astype(F32)[:FFN_HALO]], axis=0)
        return jnp.concatenate([main_ref[0], halo.astype(BF16)], axis=0)

    x_all = jnp.concatenate([x_ref[0], prev_ref[0], next_ref[0]], axis=0)
    ya = with_halo(ya_ref, ya_prev_ref, ya_next_ref)
    yb = with_halo(yb_ref, yb_prev_ref, yb_next_ref)
    n_groups = (ts + 2 * FFN_HALO) // BF16_ROWS
    cuts = [BF16_ROWS * (n_groups * i // MIX_BLOCKS) for i in range(MIX_BLOCKS + 1)]
    parts = []
    for lo, hi in zip(cuts[:-1], cuts[1:]):
        y = _dot(ya[lo:hi], wout_ref[:BRANCH, :]) + _dot(yb[lo:hi], wout_ref[BRANCH:, :])
        parts.append(x_all[lo:hi] + _rms(y, mix_g_ref[...]))
    _ffn_ple_tail(jnp.concatenate(parts, axis=0), p_ref, *rest, ts)


def _ffn_call(kernel_fn, name, front_specs, front_args, x, p, layer, pre_g, w_up, conv_w, conv_b, w_down,
              post_g, gate_g, w_g, b_g, w_p, ts):
    b, s, d = x.shape
    halo = FFN_HALO
    d_ff = w_down.shape[1]
    tail = (pre_g, w_up, conv_w, conv_b, w_down, post_g, gate_g, w_g, b_g, w_p)
    return pl.pallas_call(
        functools.partial(kernel_fn, ts=ts),
        grid=(b, s // ts),
        in_specs=_halo_specs(ts, halo, d, s) + front_specs
        + [pl.BlockSpec((None, 1, ts, p.shape[-1]), lambda i, t: (layer, i, t, 0))]
        + [_layer(c, layer) for c in tail],
        out_specs=pl.BlockSpec((1, ts, d), lambda i, t: (i, t, 0)),
        out_shape=jax.ShapeDtypeStruct(x.shape, F32),
        scratch_shapes=[
            pltpu.VMEM((2, 2 * FF_CHUNK // LANES, ROW_PITCH * (ts + 2 * halo), LANES), F32),
            pltpu.VMEM((ts, d_ff), BF16),
        ],
        compiler_params=_params(2),
        name=name,
    )(x, x, x, *front_args, p, *tail)


def _ffn_ple(x, p, layer, *tail, ts):
    return _ffn_call(_ffn_ple_kernel, "ffn_ple", [], [], x, p, layer, *tail, ts)


def _even_tail(x, ya, yb, j, w_out, mix_g, p, layer, *tail, ts):
    s = x.shape[1]
    front_specs = (_halo_specs(ts, BF16_ROWS, BRANCH, s) + _halo_specs(ts, BF16_ROWS, BRANCH, s)
                   + [_layer(w_out, j), _layer(mix_g, layer)])
    return _ffn_call(_even_tail_kernel, "even_tail", front_specs, [ya, ya, ya, yb, yb, yb, w_out, mix_g],
                     x, p, layer, *tail, ts)


def _channel_dft_kernel(cs_ref, wf_ref, o_ref):
    for g in range(GROUPS):
        for l in range(2):
            o_ref[g, :, l * LANES:(l + 1) * LANES] = jnp.dot(
                cs_ref[l], wf_ref[g], precision=lax.Precision.HIGHEST, preferred_element_type=F32).astype(BF16)


def _channel_dft_weights(w_f, s):
    n = lax.broadcasted_iota(jnp.int32, (LANES, LANES), 0) * lax.broadcasted_iota(jnp.int32, (LANES, LANES), 1)
    ang = (n % LANES).astype(F32) * (2.0 * math.pi / LANES)
    scale = 1.0 / math.sqrt(float(s) * LANES)
    cs = jnp.stack([jnp.cos(ang) * scale, -jnp.sin(ang) * scale])
    return pl.pallas_call(
        _channel_dft_kernel,
        out_shape=jax.ShapeDtypeStruct((GROUPS, LANES, 2 * LANES), BF16),
        name="channel_dft_weights",
    )(cs, w_f)


def _seq_dft_table():
    shape = (DFT_N2, DFT_N1, DFT_N1)
    n2 = lax.broadcasted_iota(jnp.int32, shape, 0)
    k1 = lax.broadcasted_iota(jnp.int32, shape, 1)
    n1 = lax.broadcasted_iota(jnp.int32, shape, 2)
    s = DFT_N1 * DFT_N2
    ang = (((n2 + DFT_N2 * n1) * k1) % s).astype(F32) * (-2.0 * math.pi / s)
    return jnp.concatenate([jnp.cos(ang), jnp.sin(ang)], axis=1).astype(BF16)


def _rows(v):
    return v[:, None, :]


def kernel(x, p, mix_pre_g, mix_post_g, ffn_pre_g, ffn_post_g, ev_w_in, ev_w_fourier, ev_v_ln_g, ev_v_ln_b, ev_w_spatial, ev_b_spatial, ev_w_out, od_w_in, od_conv_w, od_conv_b, od_ln_g, od_ln_b, od_sconv_w, od_w_out, ffn_w_up, ffn_conv_w, ffn_conv_b, ffn_w_down, ple_w_p, ple_gate_g, ple_w_g, ple_b_g):
    depth = mix_pre_g.shape[0]
    s = x.shape[1]
    ts = min(512, s)
    tab = _seq_dft_table()
    mix_pre_g, mix_post_g = _rows(mix_pre_g), _rows(mix_post_g)
    ffn_tail = (_rows(ffn_pre_g), ffn_w_up.astype(BF16), ffn_conv_w, _rows(ffn_conv_b), ffn_w_down.astype(BF16),
                _rows(ffn_post_g), _rows(ple_gate_g), ple_w_g.astype(BF16), _rows(ple_b_g), ple_w_p.astype(BF16))
    ev_w_in, ev_w_spatial, ev_w_out = ev_w_in.astype(BF16), ev_w_spatial.astype(BF16), ev_w_out.astype(BF16)
    ev_v_ln_g, ev_v_ln_b = _rows(ev_v_ln_g), _rows(ev_v_ln_b)
    od_w_in, od_w_out = od_w_in.astype(BF16), od_w_out.astype(BF16)
    od_conv_b, od_ln_g, od_ln_b = _rows(od_conv_b), _rows(od_ln_g), _rows(od_ln_b)
    for i in range(depth):
        j = i // 2
        if i % 2 == 0:
            wf = _channel_dft_weights(ev_w_fourier[j], s)
            bs = jnp.broadcast_to(ev_b_spatial[j][:, :, None], (GROUPS, CHUNK, LANES))
            z, yb = _even_in(x, i, j, mix_pre_g, ev_w_in, wf, ev_v_ln_g, ev_v_ln_b, ev_w_spatial, bs, ts)
            ya = _seq_dft(tab, z)
            x = _even_tail(x, ya, yb, j, ev_w_out, mix_post_g, p, i, *ffn_tail, ts=ts)
        else:
            x = _odd_mix(x, i, j, mix_pre_g, od_w_in, od_conv_w, od_conv_b, od_ln_g, od_ln_b, od_sconv_w,
                         od_w_out, mix_post_g, ts)
            x = _ffn_ple(x, p, i, *ffn_tail, ts=ts)
    return x
```

```python
import functools
import math

import jax
import jax.numpy as jnp
from jax import lax
from jax.experimental import pallas as pl
from jax.experimental.pallas import tpu as pltpu

EPS = 1e-6
LANES = 128
BF16_ROWS = 16
GROUPS = 4
BRANCH = GROUPS * LANES
CHUNK = 128
CONF_K = 31
CONF_HALO = 16
FFN_HALO = 8
V7X_VMEM_LIMIT = 56 * 1024 * 1024

F32 = jnp.float32
BF16 = jnp.bfloat16


def _rms(x, g):
    ms = jnp.mean(x * x, axis=-1, keepdims=True)
    return x * lax.rsqrt(ms + EPS) * g


def _layernorm(x, g, b):
    mu = jnp.mean(x, axis=-1, keepdims=True)
    xc = x - mu
    var = jnp.mean(xc * xc, axis=-1, keepdims=True)
    return xc * lax.rsqrt(var + EPS) * g + b


def _dot(a, b):
    return jnp.dot(a, b, preferred_element_type=F32)


def _resident(shape):
    nd = len(shape)
    return pl.BlockSpec(shape, lambda *_: (0,) * nd, pipeline_mode=pl.Buffered(1))


def _layer(arr, layer):
    nd = arr.ndim - 1
    return pl.BlockSpec((None,) + arr.shape[1:], lambda *_: (layer,) + (0,) * nd, pipeline_mode=pl.Buffered(1))


def _params(n_grid_axes):
    return pltpu.CompilerParams(
        dimension_semantics=("arbitrary",) * n_grid_axes,
        vmem_limit_bytes=V7X_VMEM_LIMIT,
    )


ROW_PITCH = 2


def _stage_slab(slab_ref, v, ts, halo):
    slab_ref[pl.ds(0, halo, stride=ROW_PITCH), :] = v[ts:ts + halo, :]
    slab_ref[pl.ds(ROW_PITCH * halo, ts, stride=ROW_PITCH), :] = v[0:ts, :]
    slab_ref[pl.ds(ROW_PITCH * (halo + ts), halo, stride=ROW_PITCH), :] = v[ts + halo:ts + 2 * halo, :]


def _slab_rows(slab_ref, first, n):
    return slab_ref[pl.ds(ROW_PITCH * first, n, stride=ROW_PITCH), :]


def _halo_specs(ts, halo, width, s):
    per = ts // halo
    last = s // halo - 1
    return [
        pl.BlockSpec((1, ts, width), lambda i, j: (i, j, 0)),
        pl.BlockSpec((1, halo, width), lambda i, j: (i, jnp.maximum(j * per - 1, 0), 0)),
        pl.BlockSpec((1, halo, width), lambda i, j: (i, jnp.minimum((j + 1) * per, last), 0)),
    ]


def _mask_halo(v, ts, halo):
    j = pl.program_id(1)
    row = lax.broadcasted_iota(jnp.int32, (ts + 2 * halo, 1), 0)
    outside = ((row >= ts) & (row < ts + halo) & (j == 0)) | ((row >= ts + halo) & (j == pl.num_programs(1) - 1))
    return jnp.where(outside, 0.0, v)


DFT_N1 = 256
DFT_N2 = 16
PERM_PITCH = 17


def _even_in_kernel(x_ref, g_ref, win_ref, wf_ref, lng_ref, lnb_ref, ws_ref, bs_ref,
                    z_ref, yb_ref, perm_ref, *, ts):
    h = _rms(x_ref[0], g_ref[...]).astype(BF16)
    pair = 2 * LANES
    n_blk = ts // DFT_N2

    def in_proj(q):
        return [_dot(h, win_ref[:, base + q * pair:base + (q + 1) * pair]) for base in (0, BRANCH, 2 * BRANCH)]

    def fourier_channel_stage(za, k):
        t = _dot(za.astype(BF16), wf_ref[k])
        for l in range(2):
            lanes = slice(l * LANES, (l + 1) * LANES)
            slab = perm_ref.at[2 * k + l]
            for j in range(n_blk):
                slab[pl.ds(ROW_PITCH * PERM_PITCH * j, DFT_N2, stride=ROW_PITCH), :] = (
                    t[j * DFT_N2:(j + 1) * DFT_N2, lanes])
            for n2 in range(DFT_N2):
                rows = slab[pl.ds(ROW_PITCH * n2, n_blk, stride=ROW_PITCH * PERM_PITCH), :]
                z_ref[0, n2, :, k * 2 * LANES + l * LANES:k * 2 * LANES + (l + 1) * LANES] = rows.astype(BF16)

    def spatial_gating(zu, zv, k):
        sl = slice(k * LANES, (k + 1) * LANES)
        vn = _layernorm(jax.nn.gelu(zv), lng_ref[:, sl], lnb_ref[:, sl]).astype(BF16)
        zu = jax.nn.gelu(zu)
        for c in range(ts // CHUNK):
            rows = slice(c * CHUNK, (c + 1) * CHUNK)
            sv = _dot(ws_ref[k], vn[rows, :]) + bs_ref[k]
            yb_ref[0, rows, sl] = (zu[rows, :] * sv).astype(BF16)

    z_next = in_proj(0)
    for q in range(GROUPS // 2):
        za, zu, zv = z_next
        if q + 1 < GROUPS // 2:
            z_next = in_proj(q + 1)
        for l in range(2):
            lanes = slice(l * LANES, (l + 1) * LANES)
            fourier_channel_stage(za[:, lanes], 2 * q + l)
            spatial_gating(zu[:, lanes], zv[:, lanes], 2 * q + l)


def _even_in(x, layer, j, pre_g, w_in, wf, ln_g, ln_b, ws, bs, ts):
    b, s, d = x.shape
    return pl.pallas_call(
        functools.partial(_even_in_kernel, ts=ts),
        grid=(b, s // ts),
        in_specs=[
            pl.BlockSpec((1, ts, d), lambda i, t: (i, t, 0)),
            _layer(pre_g, layer), _layer(w_in, j), _resident(wf.shape),
            _layer(ln_g, j), _layer(ln_b, j), _layer(ws, j), _resident(bs.shape),
        ],
        out_specs=[
            pl.BlockSpec((1, DFT_N2, ts // DFT_N2, 2 * BRANCH), lambda i, t: (i, 0, t, 0)),
            pl.BlockSpec((1, ts, BRANCH), lambda i, t: (i, t, 0)),
        ],
        out_shape=[
            jax.ShapeDtypeStruct((b, DFT_N2, s // DFT_N2, 2 * BRANCH), BF16),
            jax.ShapeDtypeStruct((b, s, BRANCH), BF16),
        ],
        scratch_shapes=[
            pltpu.VMEM((2 * GROUPS, ROW_PITCH * PERM_PITCH * (ts // DFT_N2), LANES), F32),
        ],
        compiler_params=_params(2),
        name="even_in",
    )(x, pre_g, w_in, wf, ln_g, ln_b, ws, bs)


DFT_ROWS = 16


def _dft16_real(br, bi):
    u = [[None] * 4 for _ in range(4)]
    for nb in range(4):
        x0r, x1r, x2r, x3r = br[nb], br[4 + nb], br[8 + nb], br[12 + nb]
        x0i, x1i, x2i, x3i = bi[nb], bi[4 + nb], bi[8 + nb], bi[12 + nb]
        t0r, t0i = x0r + x2r, x0i + x2i
        t1r, t1i = x0r - x2r, x0i - x2i
        t2r, t2i = x1r + x3r, x1i + x3i
        t3r, t3i = x1r - x3r, x1i - x3i
        u[0][nb] = (t0r + t2r, t0i + t2i)
        u[2][nb] = (t0r - t2r, t0i - t2i)
        u[1][nb] = (t1r + t3i, t1i - t3r)
        u[3][nb] = (t1r - t3i, t1i + t3r)
    out = [None] * 16
    for ka in range(4):
        vr, vi = [], []
        for nb in range(4):
            a, b = u[ka][nb]
            e = (nb * ka) % 16
            wr, wi = math.cos(-2.0 * math.pi * e / 16), math.sin(-2.0 * math.pi * e / 16)
            if e == 0:
                vr.append(a)
                vi.append(b)
            elif e == 4:
                vr.append(b)
                vi.append(-a)
            else:
                vr.append(a * wr - b * wi)
                vi.append(a * wi + b * wr if nb in (1, 3) else None)
        s0, s1, s2, s3 = vr[0] + vr[2], vr[0] - vr[2], vr[1] + vr[3], vi[1] - vi[3]
        out[ka] = s0 + s2
        out[ka + 4] = s1 + s3
        out[ka + 8] = s0 - s2
        out[ka + 12] = s1 - s3
    return out


def _seq_dft_kernel(tab_ref, z_ref, o_ref, bre_ref, bim_ref):
    def matmul_stage(g):
        for n2 in range(DFT_N2):
            t = _dot(tab_ref[n2], z_ref[0, n2, :, g * 2 * LANES:(g + 1) * 2 * LANES])
            bre_ref[g % 2, n2] = t[:DFT_N1, :LANES] - t[DFT_N1:, LANES:]
            bim_ref[g % 2, n2] = t[:DFT_N1, LANES:] + t[DFT_N1:, :LANES]

    def slab_stage(g):
        for r in range(DFT_N1 // DFT_ROWS):
            rows = slice(r * DFT_ROWS, (r + 1) * DFT_ROWS)
            y = _dft16_real([bre_ref[g % 2, n2, rows, :] for n2 in range(DFT_N2)],
                            [bim_ref[g % 2, n2, rows, :] for n2 in range(DFT_N2)])
            for k2 in range(DFT_N2):
                o_ref[0, k2 * DFT_N1 + r * DFT_ROWS:k2 * DFT_N1 + (r + 1) * DFT_ROWS,
                      g * LANES:(g + 1) * LANES] = y[k2].astype(BF16)

    matmul_stage(0)
    for g in range(GROUPS):
        if g + 1 < GROUPS:
            matmul_stage(g + 1)
        slab_stage(g)


def _seq_dft(tab, z):
    b, n2, n1, width = z.shape
    assert (n2, n1) == (DFT_N2, DFT_N1), "the factored sequence DFT is written for S = 4096"
    return pl.pallas_call(
        _seq_dft_kernel,
        grid=(b,),
        in_specs=[
            _resident(tab.shape),
            pl.BlockSpec((1, n2, n1, width), lambda i: (i, 0, 0, 0)),
        ],
        out_specs=pl.BlockSpec((1, n1 * n2, BRANCH), lambda i: (i, 0, 0)),
        out_shape=jax.ShapeDtypeStruct((b, n1 * n2, BRANCH), BF16),
        scratch_shapes=[
            pltpu.VMEM((2, n2, n1, LANES), F32),
            pltpu.VMEM((2, n2, n1, LANES), F32),
        ],
        compiler_params=_params(1),
        name="seq_dft",
    )(tab, z)


CONV_ROWS = 64
ODD_BLOCKS = 2


def _odd_mix_kernel(x_ref, prev_ref, next_ref, pre_g_ref, win_ref, cw_ref, cb_ref, lng_ref, lnb_ref,
                    sw_ref, wout_ref, post_g_ref, o_ref, gslab_ref, mslab_ref, *, ts):
    halo = CONF_HALO
    pair = 2 * LANES
    x_all = jnp.concatenate([x_ref[0], prev_ref[0], next_ref[0]], axis=0)
    h = _mask_halo(_rms(x_all, pre_g_ref[...]), ts, halo).astype(BF16)

    def proj(first_col, width):
        return _dot(h, win_ref[:, first_col:first_col + width])

    def glu_conv_norm(a, gate, first_group):
        glu = a * jax.nn.sigmoid(gate)
        outs = []
        for l in range(2):
            k = first_group + l
            sl = slice(k * LANES, (k + 1) * LANES)
            slab = gslab_ref.at[k]
            _stage_slab(slab, glu[:, l * LANES:(l + 1) * LANES], ts, halo)
            first = halo - CONF_K // 2
            blocks = []
            for r in range(ts // CONV_ROWS):
                r0 = r * CONV_ROWS + first
                acc = _slab_rows(slab, r0, CONV_ROWS) * cw_ref[0:1, sl]
                for t in range(1, CONF_K):
                    acc = acc + _slab_rows(slab, r0 + t, CONV_ROWS) * cw_ref[t:t + 1, sl]
                blocks.append(acc)
            c = jnp.concatenate(blocks, axis=0) + cb_ref[:, sl]
            outs.append(jax.nn.silu(_layernorm(c, lng_ref[:, sl], lnb_ref[:, sl])))
        return jnp.concatenate(outs, axis=-1).astype(BF16)

    a0, g0 = proj(0, pair), proj(BRANCH, pair)
    a1, g1 = proj(pair, pair), proj(BRANCH + pair, pair)
    yc0 = glu_conv_norm(a0, g0, 0)
    bg = proj(2 * BRANCH, BRANCH)[:ts]
    cg = proj(3 * BRANCH, BRANCH)
    xin = proj(4 * BRANCH, BRANCH)
    yc1 = glu_conv_norm(a1, g1, 2)
    m = cg * xin
    yd = []
    for k in range(GROUPS):
        sl = slice(k * LANES, (k + 1) * LANES)
        slab = mslab_ref.at[k]
        _stage_slab(slab, m[:, sl], ts, halo)
        conv = (_slab_rows(slab, halo - 1, ts) * sw_ref[0:1, sl] + m[0:ts, sl] * sw_ref[1:2, sl]
                + _slab_rows(slab, halo + 1, ts) * sw_ref[2:3, sl])
        yd.append(bg[:, sl] * conv)
    ydc = jnp.concatenate(yd, axis=-1).astype(BF16)
    blk = ts // ODD_BLOCKS
    for i in range(ODD_BLOCKS):
        rows = slice(i * blk, (i + 1) * blk)
        y = (_dot(yc0[rows], wout_ref[0:pair, :]) + _dot(yc1[rows], wout_ref[pair:BRANCH, :])
             + _dot(ydc[rows], wout_ref[BRANCH:, :]))
        o_ref[0, rows, :] = x_ref[0, rows, :] + _rms(y, post_g_ref[...])


def _odd_mix(x, layer, j, pre_g, w_in, conv_w, conv_b, ln_g, ln_b, sconv_w, w_out, post_g, ts):
    b, s, d = x.shape
    halo = CONF_HALO
    return pl.pallas_call(
        functools.partial(_odd_mix_kernel, ts=ts),
        grid=(b, s // ts),
        in_specs=_halo_specs(ts, halo, d, s) + [
            _layer(pre_g, layer), _layer(w_in, j), _layer(conv_w, j), _layer(conv_b, j), _layer(ln_g, j),
            _layer(ln_b, j), _layer(sconv_w, j), _layer(w_out, j), _layer(post_g, layer)],
        out_specs=pl.BlockSpec((1, ts, d), lambda i, t: (i, t, 0)),
        out_shape=jax.ShapeDtypeStruct(x.shape, F32),
        scratch_shapes=[
            pltpu.VMEM((GROUPS, ROW_PITCH * (ts + 2 * halo), LANES), F32),
            pltpu.VMEM((GROUPS, ROW_PITCH * (ts + 2 * halo), LANES), F32),
        ],
        compiler_params=_params(2),
        name="odd_mix",
    )(x, x, x, pre_g, w_in, conv_w, conv_b, ln_g, ln_b, sconv_w, w_out, post_g)


FF_CHUNK = 256
TAIL_BLOCKS = 4
MIX_BLOCKS = 3


def _ffn_ple_tail(x1_all, p_ref, pre_g_ref, wup_ref, cw_ref, cb_ref, wdown_ref, post_g_ref, gate_g_ref,
                  wg_ref, bg_ref, wp_ref, o_ref, ext_ref, act_ref, ts):
    halo = FFN_HALO
    d_ff = wdown_ref.shape[0]
    n_slab = FF_CHUNK // LANES
    n_chunks = d_ff // FF_CHUNK
    h = _mask_halo(_rms(x1_all, pre_g_ref[...]), ts, halo).astype(BF16)

    def up_proj(c):
        return [_dot(h, wup_ref[:, base + c * FF_CHUNK:base + (c + 1) * FF_CHUNK]) for base in (0, d_ff)]

    z_next = up_proj(0)
    for c in range(n_chunks):
        z_cur = z_next
        if c + 1 < n_chunks:
            z_next = up_proj(c + 1)
        halves = []
        for half, base in enumerate((0, d_ff)):
            cols = slice(base + c * FF_CHUNK, base + (c + 1) * FF_CHUNK)
            zc = z_cur[half]
            w = cw_ref[:, cols]
            bias = cb_ref[:, cols]
            outs = []
            for l in range(n_slab):
                lanes = slice(l * LANES, (l + 1) * LANES)
                slab = ext_ref.at[c % 2, half * n_slab + l]
                _stage_slab(slab, zc[:, lanes], ts, halo)
                outs.append(_slab_rows(slab, halo - 1, ts) * w[0:1, lanes]
                            + zc[0:ts, lanes] * w[1:2, lanes]
                            + _slab_rows(slab, halo + 1, ts) * w[2:3, lanes]
                            + bias[:, lanes])
            halves.append(jnp.concatenate(outs, axis=-1))
        act_ref[:, c * FF_CHUNK:(c + 1) * FF_CHUNK] = (jax.nn.gelu(halves[0]) * halves[1]).astype(BF16)
    blk = ts // TAIL_BLOCKS
    fs = [_dot(act_ref[i * blk:(i + 1) * blk, :], wdown_ref[...]) for i in range(TAIL_BLOCKS)]
    emb = _dot(p_ref[0].astype(BF16), wp_ref[...])
    for i, f in enumerate(fs):
        rows = slice(i * blk, (i + 1) * blk)
        x2 = x1_all[rows] + _rms(f, post_g_ref[...])
        hg = _rms(x2, gate_g_ref[...]).astype(BF16)
        gate = jax.nn.sigmoid(_dot(hg, wg_ref[...]) + bg_ref[...])
        o_ref[0, rows, :] = x2 + gate * emb[rows]


def _ffn_ple_kernel(x_ref, prev_ref, next_ref, p_ref, *rest, ts):
    x1_all = jnp.concatenate([x_ref[0], prev_ref[0], next_ref[0]], axis=0)
    _ffn_ple_tail(x1_all, p_ref, *rest, ts)


def _even_tail_kernel(x_ref, prev_ref, next_ref, ya_ref, ya_prev_ref, ya_next_ref, yb_ref, yb_prev_ref,
                      yb_next_ref, wout_ref, mix_g_ref, p_ref, *rest, ts):
    tail_of_block = BF16_ROWS - FFN_HALO

    def with_halo(main_ref, before_ref, after_ref):
        halo = jnp.concatenate([before_ref[0].astype(F32)[tail_of_block:], after_ref[0].astype(F32)[:FFN_HALO]], axis=0)
        return jnp.concatenate([main_ref[0], halo.astype(BF16)], axis=0)

    x_all = jnp.concatenate([x_ref[0], prev_ref[0], next_ref[0]], axis=0)
    ya = with_halo(ya_ref, ya_prev_ref, ya_next_ref)
    yb = with_halo(yb_ref, yb_prev_ref, yb_next_ref)
    n_groups = (ts + 2 * FFN_HALO) // BF16_ROWS
    cuts = [BF16_ROWS * (n_groups * i // MIX_BLOCKS) for i in range(MIX_BLOCKS + 1)]
    parts = []
    for lo, hi in zip(cuts[:-1], cuts[1:]):
        y = _dot(ya[lo:hi], wout_ref[:BRANCH, :]) + _dot(yb[lo:hi], wout_ref[BRANCH:, :])
        parts.append(x_all[lo:hi] + _rms(y, mix_g_ref[...]))
    _ffn_ple_tail(jnp.concatenate(parts, axis=0), p_ref, *rest, ts)


def _ffn_call(kernel_fn, name, front_specs, front_args, x, p, layer, pre_g, w_up, conv_w, conv_b, w_down,
              post_g, gate_g, w_g, b_g, w_p, ts):
    b, s, d = x.shape
    halo = FFN_HALO
    d_ff = w_down.shape[1]
    tail = (pre_g, w_up, conv_w, conv_b, w_down, post_g, gate_g, w_g, b_g, w_p)
    return pl.pallas_call(
        functools.partial(kernel_fn, ts=ts),
        grid=(b, s // ts),
        in_specs=_halo_specs(ts, halo, d, s) + front_specs
        + [pl.BlockSpec((None, 1, ts, p.shape[-1]), lambda i, t: (layer, i, t, 0))]
        + [_layer(c, layer) for c in tail],
        out_specs=pl.BlockSpec((1, ts, d), lambda i, t: (i, t, 0)),
        out_shape=jax.ShapeDtypeStruct(x.shape, F32),
        scratch_shapes=[
            pltpu.VMEM((2, 2 * FF_CHUNK // LANES, ROW_PITCH * (ts + 2 * halo), LANES), F32),
            pltpu.VMEM((ts, d_ff), BF16),
        ],
        compiler_params=_params(2),
        name=name,
    )(x, x, x, *front_args, p, *tail)


def _ffn_ple(x, p, layer, *tail, ts):
    return _ffn_call(_ffn_ple_kernel, "ffn_ple", [], [], x, p, layer, *tail, ts)


def _even_tail(x, ya, yb, j, w_out, mix_g, p, layer, *tail, ts):
    s = x.shape[1]
    front_specs = (_halo_specs(ts, BF16_ROWS, BRANCH, s) + _halo_specs(ts, BF16_ROWS, BRANCH, s)
                   + [_layer(w_out, j), _layer(mix_g, layer)])
    return _ffn_call(_even_tail_kernel, "even_tail", front_specs, [ya, ya, ya, yb, yb, yb, w_out, mix_g],
                     x, p, layer, *tail, ts)


def _channel_dft_kernel(cs_ref, wf_ref, o_ref):
    for g in range(GROUPS):
        for l in range(2):
            o_ref[g, :, l * LANES:(l + 1) * LANES] = jnp.dot(
                cs_ref[l], wf_ref[g], precision=lax.Precision.HIGHEST, preferred_element_type=F32).astype(BF16)


def _channel_dft_weights(w_f, s):
    n = lax.broadcasted_iota(jnp.int32, (LANES, LANES), 0) * lax.broadcasted_iota(jnp.int32, (LANES, LANES), 1)
    ang = (n % LANES).astype(F32) * (2.0 * math.pi / LANES)
    scale = 1.0 / math.sqrt(float(s) * LANES)
    cs = jnp.stack([jnp.cos(ang) * scale, -jnp.sin(ang) * scale])
    return pl.pallas_call(
        _channel_dft_kernel,
        out_shape=jax.ShapeDtypeStruct((GROUPS, LANES, 2 * LANES), BF16),
        name="channel_dft_weights",
    )(cs, w_f)


def _seq_dft_table():
    shape = (DFT_N2, DFT_N1, DFT_N1)
    n2 = lax.broadcasted_iota(jnp.int32, shape, 0)
    k1 = lax.broadcasted_iota(jnp.int32, shape, 1)
    n1 = lax.broadcasted_iota(jnp.int32, shape, 2)
    s = DFT_N1 * DFT_N2
    ang = (((n2 + DFT_N2 * n1) * k1) % s).astype(F32) * (-2.0 * math.pi / s)
    return jnp.concatenate([jnp.cos(ang), jnp.sin(ang)], axis=1).astype(BF16)


def _rows(v):
    return v[:, None, :]


def kernel(x, p, mix_pre_g, mix_post_g, ffn_pre_g, ffn_post_g, ev_w_in, ev_w_fourier, ev_v_ln_g, ev_v_ln_b, ev_w_spatial, ev_b_spatial, ev_w_out, od_w_in, od_conv_w, od_conv_b, od_ln_g, od_ln_b, od_sconv_w, od_w_out, ffn_w_up, ffn_conv_w, ffn_conv_b, ffn_w_down, ple_w_p, ple_gate_g, ple_w_g, ple_b_g):
    depth = mix_pre_g.shape[0]
    s = x.shape[1]
    ts = min(512, s)
    tab = _seq_dft_table()
    mix_pre_g, mix_post_g = _rows(mix_pre_g), _rows(mix_post_g)
    ffn_tail = (_rows(ffn_pre_g), ffn_w_up.astype(BF16), ffn_conv_w, _rows(ffn_conv_b), ffn_w_down.astype(BF16),
                _rows(ffn_post_g), _rows(ple_gate_g), ple_w_g.astype(BF16), _rows(ple_b_g), ple_w_p.astype(BF16))
    ev_w_in, ev_w_spatial, ev_w_out = ev_w_in.astype(BF16), ev_w_spatial.astype(BF16), ev_w_out.astype(BF16)
    ev_v_ln_g, ev_v_ln_b = _rows(ev_v_ln_g), _rows(ev_v_ln_b)
    od_w_in, od_w_out = od_w_in.astype(BF16), od_w_out.astype(BF16)
    od_conv_b, od_ln_g, od_ln_b = _rows(od_conv_b), _rows(od_ln_g), _rows(od_ln_b)
    for i in range(depth):
        j = i // 2
        if i % 2 == 0:
            wf = _channel_dft_weights(ev_w_fourier[j], s)
            bs = jnp.broadcast_to(ev_b_spatial[j][:, :, None], (GROUPS, CHUNK, LANES))
            z, yb = _even_in(x, i, j, mix_pre_g, ev_w_in, wf, ev_v_ln_g, ev_v_ln_b, ev_w_spatial, bs, ts)
            ya = _seq_dft(tab, z)
            x = _even_tail(x, ya, yb, j, ev_w_out, mix_post_g, p, i, *ffn_tail, ts=ts)
        else:
            x = _odd_mix(x, i, j, mix_pre_g, od_w_in, od_conv_w, od_conv_b, od_ln_g, od_ln_b, od_sconv_w,
                         od_w_out, mix_post_g, ts)
            x = _ffn_ple(x, p, i, *ffn_tail, ts=ts)
    return x
```

```python
import functools
import math

import jax
import jax.numpy as jnp
from jax import lax
from jax.experimental import pallas as pl
from jax.experimental.pallas import tpu as pltpu

EPS = 1e-6
LANES = 128
BF16_ROWS = 16
GROUPS = 4
BRANCH = GROUPS * LANES
CHUNK = 128
CONF_K = 31
CONF_HALO = 16
FFN_HALO = 8
V7X_VMEM_LIMIT = 56 * 1024 * 1024

F32 = jnp.float32
BF16 = jnp.bfloat16


def _rms(x, g):
    ms = jnp.mean(x * x, axis=-1, keepdims=True)
    return x * lax.rsqrt(ms + EPS) * g


def _layernorm(x, g, b):
    mu = jnp.mean(x, axis=-1, keepdims=True)
    xc = x - mu
    var = jnp.mean(xc * xc, axis=-1, keepdims=True)
    return xc * lax.rsqrt(var + EPS) * g + b


def _dot(a, b):
    return jnp.dot(a, b, preferred_element_type=F32)


def _resident(shape):
    nd = len(shape)
    return pl.BlockSpec(shape, lambda *_: (0,) * nd, pipeline_mode=pl.Buffered(1))


def _layer(arr, layer):
    nd = arr.ndim - 1
    return pl.BlockSpec((None,) + arr.shape[1:], lambda *_: (layer,) + (0,) * nd, pipeline_mode=pl.Buffered(1))


def _params(n_grid_axes):
    return pltpu.CompilerParams(
        dimension_semantics=("parallel",) + ("arbitrary",) * (n_grid_axes - 1),
        vmem_limit_bytes=V7X_VMEM_LIMIT,
    )


ROW_PITCH = 2


def _stage_slab(slab_ref, v, ts, halo):
    slab_ref[pl.ds(0, halo, stride=ROW_PITCH), :] = v[ts:ts + halo, :]
    slab_ref[pl.ds(ROW_PITCH * halo, ts, stride=ROW_PITCH), :] = v[0:ts, :]
    slab_ref[pl.ds(ROW_PITCH * (halo + ts), halo, stride=ROW_PITCH), :] = v[ts + halo:ts + 2 * halo, :]


def _slab_rows(slab_ref, first, n):
    return slab_ref[pl.ds(ROW_PITCH * first, n, stride=ROW_PITCH), :]


def _halo_specs(ts, halo, width, s):
    per = ts // halo
    last = s // halo - 1
    return [
        pl.BlockSpec((1, ts, width), lambda i, j: (i, j, 0)),
        pl.BlockSpec((1, halo, width), lambda i, j: (i, jnp.maximum(j * per - 1, 0), 0)),
        pl.BlockSpec((1, halo, width), lambda i, j: (i, jnp.minimum((j + 1) * per, last), 0)),
    ]


def _mask_halo(v, ts, halo):
    j = pl.program_id(1)
    row = lax.broadcasted_iota(jnp.int32, (ts + 2 * halo, 1), 0)
    outside = ((row >= ts) & (row < ts + halo) & (j == 0)) | ((row >= ts + halo) & (j == pl.num_programs(1) - 1))
    return jnp.where(outside, 0.0, v)


DFT_N1 = 256
DFT_N2 = 16
PERM_PITCH = 17


def _even_in_kernel(x_ref, g_ref, win_ref, wf_ref, lng_ref, lnb_ref, ws_ref, bs_ref,
                    z_ref, yb_ref, perm_ref, *, ts):
    h = _rms(x_ref[0], g_ref[...]).astype(BF16)
    pair = 2 * LANES
    n_blk = ts // DFT_N2

    def in_proj(q):
        return [_dot(h, win_ref[:, base + q * pair:base + (q + 1) * pair]) for base in (0, BRANCH, 2 * BRANCH)]

    def fourier_channel_stage(za, k):
        t = _dot(za.astype(BF16), wf_ref[k])
        for l in range(2):
            lanes = slice(l * LANES, (l + 1) * LANES)
            slab = perm_ref.at[2 * k + l]
            for j in range(n_blk):
                slab[pl.ds(ROW_PITCH * PERM_PITCH * j, DFT_N2, stride=ROW_PITCH), :] = (
                    t[j * DFT_N2:(j + 1) * DFT_N2, lanes])
            for n2 in range(DFT_N2):
                rows = slab[pl.ds(ROW_PITCH * n2, n_blk, stride=ROW_PITCH * PERM_PITCH), :]
                z_ref[0, n2, :, k * 2 * LANES + l * LANES:k * 2 * LANES + (l + 1) * LANES] = rows.astype(BF16)

    def spatial_gating(zu, zv, k):
        sl = slice(k * LANES, (k + 1) * LANES)
        vn = _layernorm(jax.nn.gelu(zv), lng_ref[:, sl], lnb_ref[:, sl]).astype(BF16)
        zu = jax.nn.gelu(zu)
        for c in range(ts // CHUNK):
            rows = slice(c * CHUNK, (c + 1) * CHUNK)
            sv = _dot(ws_ref[k], vn[rows, :]) + bs_ref[k]
            yb_ref[0, rows, sl] = (zu[rows, :] * sv).astype(BF16)

    z_next = in_proj(0)
    for q in range(GROUPS // 2):
        za, zu, zv = z_next
        if q + 1 < GROUPS // 2:
            z_next = in_proj(q + 1)
        for l in range(2):
            lanes = slice(l * LANES, (l + 1) * LANES)
            fourier_channel_stage(za[:, lanes], 2 * q + l)
            spatial_gating(zu[:, lanes], zv[:, lanes], 2 * q + l)


def _even_in(x, layer, j, pre_g, w_in, wf, ln_g, ln_b, ws, bs, ts):
    b, s, d = x.shape
    return pl.pallas_call(
        functools.partial(_even_in_kernel, ts=ts),
        grid=(b, s // ts),
        in_specs=[
            pl.BlockSpec((1, ts, d), lambda i, t: (i, t, 0)),
            _layer(pre_g, layer), _layer(w_in, j), _resident(wf.shape),
            _layer(ln_g, j), _layer(ln_b, j), _layer(ws, j), _resident(bs.shape),
        ],
        out_specs=[
            pl.BlockSpec((1, DFT_N2, ts // DFT_N2, 2 * BRANCH), lambda i, t: (i, 0, t, 0)),
            pl.BlockSpec((1, ts, BRANCH), lambda i, t: (i, t, 0)),
        ],
        out_shape=[
            jax.ShapeDtypeStruct((b, DFT_N2, s // DFT_N2, 2 * BRANCH), BF16),
            jax.ShapeDtypeStruct((b, s, BRANCH), BF16),
        ],
        scratch_shapes=[
            pltpu.VMEM((2 * GROUPS, ROW_PITCH * PERM_PITCH * (ts // DFT_N2), LANES), F32),
        ],
        compiler_params=_params(2),
        name="even_in",
    )(x, pre_g, w_in, wf, ln_g, ln_b, ws, bs)


DFT_ROWS = 16


def _dft16_real(br, bi):
    u = [[None] * 4 for _ in range(4)]
    for nb in range(4):
        x0r, x1r, x2r, x3r = br[nb], br[4 + nb], br[8 + nb], br[12 + nb]
        x0i, x1i, x2i, x3i = bi[nb], bi[4 + nb], bi[8 + nb], bi[12 + nb]
        t0r, t0i = x0r + x2r, x0i + x2i
        t1r, t1i = x0r - x2r, x0i - x2i
        t2r, t2i = x1r + x3r, x1i + x3i
        t3r, t3i = x1r - x3r, x1i - x3i
        u[0][nb] = (t0r + t2r, t0i + t2i)
        u[2][nb] = (t0r - t2r, t0i - t2i)
        u[1][nb] = (t1r + t3i, t1i - t3r)
        u[3][nb] = (t1r - t3i, t1i + t3r)
    out = [None] * 16
    for ka in range(4):
        vr, vi = [], []
        for nb in range(4):
            a, b = u[ka][nb]
            e = (nb * ka) % 16
            wr, wi = math.cos(-2.0 * math.pi * e / 16), math.sin(-2.0 * math.pi * e / 16)
            if e == 0:
                vr.append(a)
                vi.append(b)
            elif e == 4:
                vr.append(b)
                vi.append(-a)
            else:
                vr.append(a * wr - b * wi)
                vi.append(a * wi + b * wr if nb in (1, 3) else None)
        s0, s1, s2, s3 = vr[0] + vr[2], vr[0] - vr[2], vr[1] + vr[3], vi[1] - vi[3]
        out[ka] = s0 + s2
        out[ka + 4] = s1 + s3
        out[ka + 8] = s0 - s2
        out[ka + 12] = s1 - s3
    return out


def _seq_dft_kernel(tab_ref, z_ref, o_ref, bre_ref, bim_ref):
    def matmul_stage(g):
        for n2 in range(DFT_N2):
            t = _dot(tab_ref[n2], z_ref[0, n2, :, g * 2 * LANES:(g + 1) * 2 * LANES])
            bre_ref[g % 2, n2] = t[:DFT_N1, :LANES] - t[DFT_N1:, LANES:]
            bim_ref[g % 2, n2] = t[:DFT_N1, LANES:] + t[DFT_N1:, :LANES]

    def slab_stage(g):
        for r in range(DFT_N1 // DFT_ROWS):
            rows = slice(r * DFT_ROWS, (r + 1) * DFT_ROWS)
            y = _dft16_real([bre_ref[g % 2, n2, rows, :] for n2 in range(DFT_N2)],
                            [bim_ref[g % 2, n2, rows, :] for n2 in range(DFT_N2)])
            for k2 in range(DFT_N2):
                o_ref[0, k2 * DFT_N1 + r * DFT_ROWS:k2 * DFT_N1 + (r + 1) * DFT_ROWS,
                      g * LANES:(g + 1) * LANES] = y[k2].astype(BF16)

    matmul_stage(0)
    for g in range(GROUPS):
        if g + 1 < GROUPS:
            matmul_stage(g + 1)
        slab_stage(g)


def _seq_dft(tab, z):
    b, n2, n1, width = z.shape
    assert (n2, n1) == (DFT_N2, DFT_N1), "the factored sequence DFT is written for S = 4096"
    return pl.pallas_call(
        _seq_dft_kernel,
        grid=(b,),
        in_specs=[
            _resident(tab.shape),
            pl.BlockSpec((1, n2, n1, width), lambda i: (i, 0, 0, 0)),
        ],
        out_specs=pl.BlockSpec((1, n1 * n2, BRANCH), lambda i: (i, 0, 0)),
        out_shape=jax.ShapeDtypeStruct((b, n1 * n2, BRANCH), BF16),
        scratch_shapes=[
            pltpu.VMEM((2, n2, n1, LANES), F32),
            pltpu.VMEM((2, n2, n1, LANES), F32),
        ],
        compiler_params=_params(1),
        name="seq_dft",
    )(tab, z)


CONV_ROWS = 64


def _odd_mix_kernel(x_ref, prev_ref, next_ref, pre_g_ref, win_ref, cw_ref, cb_ref, lng_ref, lnb_ref,
                    sw_ref, wout_ref, post_g_ref, o_ref, gslab_ref, mslab_ref, *, ts):
    halo = CONF_HALO
    pair = 2 * LANES
    x_all = jnp.concatenate([x_ref[0], prev_ref[0], next_ref[0]], axis=0)
    h = _mask_halo(_rms(x_all, pre_g_ref[...]), ts, halo).astype(BF16)

    def proj(first_col, width):
        return _dot(h, win_ref[:, first_col:first_col + width])

    def glu_conv_norm(a, gate, first_group):
        glu = a * jax.nn.sigmoid(gate)
        outs = []
        for l in range(2):
            k = first_group + l
            sl = slice(k * LANES, (k + 1) * LANES)
            slab = gslab_ref.at[k]
            _stage_slab(slab, glu[:, l * LANES:(l + 1) * LANES], ts, halo)
            first = halo - CONF_K // 2
            blocks = []
            for r in range(ts // CONV_ROWS):
                r0 = r * CONV_ROWS + first
                acc = _slab_rows(slab, r0, CONV_ROWS) * cw_ref[0:1, sl]
                for t in range(1, CONF_K):
                    acc = acc + _slab_rows(slab, r0 + t, CONV_ROWS) * cw_ref[t:t + 1, sl]
                blocks.append(acc)
            c = jnp.concatenate(blocks, axis=0) + cb_ref[:, sl]
            outs.append(jax.nn.silu(_layernorm(c, lng_ref[:, sl], lnb_ref[:, sl])))
        return jnp.concatenate(outs, axis=-1).astype(BF16)

    a0, g0 = proj(0, pair), proj(BRANCH, pair)
    a1, g1 = proj(pair, pair), proj(BRANCH + pair, pair)
    yc0 = glu_conv_norm(a0, g0, 0)
    bg = proj(2 * BRANCH, BRANCH)[:ts]
    cg = proj(3 * BRANCH, BRANCH)
    xin = proj(4 * BRANCH, BRANCH)
    yc1 = glu_conv_norm(a1, g1, 2)
    y = _dot(yc0, wout_ref[0:pair, :]) + _dot(yc1, wout_ref[pair:BRANCH, :])
    m = cg * xin
    yd = []
    for k in range(GROUPS):
        sl = slice(k * LANES, (k + 1) * LANES)
        slab = mslab_ref.at[k]
        _stage_slab(slab, m[:, sl], ts, halo)
        conv = (_slab_rows(slab, halo - 1, ts) * sw_ref[0:1, sl] + m[0:ts, sl] * sw_ref[1:2, sl]
                + _slab_rows(slab, halo + 1, ts) * sw_ref[2:3, sl])
        yd.append(bg[:, sl] * conv)
    y = y + _dot(jnp.concatenate(yd, axis=-1).astype(BF16), wout_ref[BRANCH:, :])
    o_ref[0] = x_ref[0] + _rms(y, post_g_ref[...])


def _odd_mix(x, layer, j, pre_g, w_in, conv_w, conv_b, ln_g, ln_b, sconv_w, w_out, post_g, ts):
    b, s, d = x.shape
    halo = CONF_HALO
    return pl.pallas_call(
        functools.partial(_odd_mix_kernel, ts=ts),
        grid=(b, s // ts),
        in_specs=_halo_specs(ts, halo, d, s) + [
            _layer(pre_g, layer), _layer(w_in, j), _layer(conv_w, j), _layer(conv_b, j), _layer(ln_g, j),
            _layer(ln_b, j), _layer(sconv_w, j), _layer(w_out, j), _layer(post_g, layer)],
        out_specs=pl.BlockSpec((1, ts, d), lambda i, t: (i, t, 0)),
        out_shape=jax.ShapeDtypeStruct(x.shape, F32),
        scratch_shapes=[
            pltpu.VMEM((GROUPS, ROW_PITCH * (ts + 2 * halo), LANES), F32),
            pltpu.VMEM((GROUPS, ROW_PITCH * (ts + 2 * halo), LANES), F32),
        ],
        compiler_params=_params(2),
        name="odd_mix",
    )(x, x, x, pre_g, w_in, conv_w, conv_b, ln_g, ln_b, sconv_w, w_out, post_g)


FF_CHUNK = 256
TAIL_BLOCKS = 4
MIX_BLOCKS = 3


def _ffn_ple_tail(x1_all, p_ref, pre_g_ref, wup_ref, cw_ref, cb_ref, wdown_ref, post_g_ref, gate_g_ref,
                  wg_ref, bg_ref, wp_ref, o_ref, ext_ref, act_ref, ts):
    halo = FFN_HALO
    d_ff = wdown_ref.shape[0]
    n_slab = FF_CHUNK // LANES
    n_chunks = d_ff // FF_CHUNK
    h = _mask_halo(_rms(x1_all, pre_g_ref[...]), ts, halo).astype(BF16)

    def up_proj(c):
        return [_dot(h, wup_ref[:, base + c * FF_CHUNK:base + (c + 1) * FF_CHUNK]) for base in (0, d_ff)]

    z_next = up_proj(0)
    for c in range(n_chunks):
        z_cur = z_next
        if c + 1 < n_chunks:
            z_next = up_proj(c + 1)
        halves = []
        for half, base in enumerate((0, d_ff)):
            cols = slice(base + c * FF_CHUNK, base + (c + 1) * FF_CHUNK)
            zc = z_cur[half]
            w = cw_ref[:, cols]
            bias = cb_ref[:, cols]
            outs = []
            for l in range(n_slab):
                lanes = slice(l * LANES, (l + 1) * LANES)
                slab = ext_ref.at[c % 2, half * n_slab + l]
                _stage_slab(slab, zc[:, lanes], ts, halo)
                outs.append(_slab_rows(slab, halo - 1, ts) * w[0:1, lanes]
                            + zc[0:ts, lanes] * w[1:2, lanes]
                            + _slab_rows(slab, halo + 1, ts) * w[2:3, lanes]
                            + bias[:, lanes])
            halves.append(jnp.concatenate(outs, axis=-1))
        act_ref[:, c * FF_CHUNK:(c + 1) * FF_CHUNK] = (jax.nn.gelu(halves[0]) * halves[1]).astype(BF16)
    blk = ts // TAIL_BLOCKS
    fs = [_dot(act_ref[i * blk:(i + 1) * blk, :], wdown_ref[...]) for i in range(TAIL_BLOCKS)]
    emb = _dot(p_ref[0].astype(BF16), wp_ref[...])
    for i, f in enumerate(fs):
        rows = slice(i * blk, (i + 1) * blk)
        x2 = x1_all[rows] + _rms(f, post_g_ref[...])
        hg = _rms(x2, gate_g_ref[...]).astype(BF16)
        gate = jax.nn.sigmoid(_dot(hg, wg_ref[...]) + bg_ref[...])
        o_ref[0, rows, :] = x2 + gate * emb[rows]


def _ffn_ple_kernel(x_ref, prev_ref, next_ref, p_ref, *rest, ts):
    x1_all = jnp.concatenate([x_ref[0], prev_ref[0], next_ref[0]], axis=0)
    _ffn_ple_tail(x1_all, p_ref, *rest, ts)


def _even_tail_kernel(x_ref, prev_ref, next_ref, ya_ref, ya_prev_ref, ya_next_ref, yb_ref, yb_prev_ref,
                      yb_next_ref, wout_ref, mix_g_ref, p_ref, *rest, ts):
    lo = BF16_ROWS - FFN_HALO

    def with_halo(main_ref, before_ref, after_ref):
        halo = jnp.concatenate([before_ref[0].astype(F32)[lo:], after_ref[0].astype(F32)[:FFN_HALO]], axis=0)
        return jnp.concatenate([main_ref[0], halo.astype(BF16)], axis=0)

    x_all = jnp.concatenate([x_ref[0], prev_ref[0], next_ref[0]], axis=0)
    ya = with_halo(ya_ref, ya_prev_ref, ya_next_ref)
    yb = with_halo(yb_ref, yb_prev_ref, yb_next_ref)
    n_groups = (ts + 2 * FFN_HALO) // BF16_ROWS
    cuts = [BF16_ROWS * (n_groups * i // MIX_BLOCKS) for i in range(MIX_BLOCKS + 1)]
    parts = []
    for lo, hi in zip(cuts[:-1], cuts[1:]):
        y = _dot(ya[lo:hi], wout_ref[:BRANCH, :]) + _dot(yb[lo:hi], wout_ref[BRANCH:, :])
        parts.append(x_all[lo:hi] + _rms(y, mix_g_ref[...]))
    _ffn_ple_tail(jnp.concatenate(parts, axis=0), p_ref, *rest, ts)


def _ffn_call(kernel_fn, name, front_specs, front_args, x, p, layer, pre_g, w_up, conv_w, conv_b, w_down,
              post_g, gate_g, w_g, b_g, w_p, ts):
    b, s, d = x.shape
    halo = FFN_HALO
    d_ff = w_down.shape[1]
    tail = (pre_g, w_up, conv_w, conv_b, w_down, post_g, gate_g, w_g, b_g, w_p)
    return pl.pallas_call(
        functools.partial(kernel_fn, ts=ts),
        grid=(b, s // ts),
        in_specs=_halo_specs(ts, halo, d, s) + front_specs
        + [pl.BlockSpec((None, 1, ts, p.shape[-1]), lambda i, t: (layer, i, t, 0))]
        + [_layer(c, layer) for c in tail],
        out_specs=pl.BlockSpec((1, ts, d), lambda i, t: (i, t, 0)),
        out_shape=jax.ShapeDtypeStruct(x.shape, F32),
        scratch_shapes=[
            pltpu.VMEM((2, 2 * FF_CHUNK // LANES, ROW_PITCH * (ts + 2 * halo), LANES), F32),
            pltpu.VMEM((ts, d_ff), BF16),
        ],
        compiler_params=_params(2),
        name=name,
    )(x, x, x, *front_args, p, *tail)


def _ffn_ple(x, p, layer, *tail, ts):
    return _ffn_call(_ffn_ple_kernel, "ffn_ple", [], [], x, p, layer, *tail, ts)


def _even_tail(x, ya, yb, j, w_out, mix_g, p, layer, *tail, ts):
    s = x.shape[1]
    front_specs = (_halo_specs(ts, BF16_ROWS, BRANCH, s) + _halo_specs(ts, BF16_ROWS, BRANCH, s)
                   + [_layer(w_out, j), _layer(mix_g, layer)])
    return _ffn_call(_even_tail_kernel, "even_tail", front_specs, [ya, ya, ya, yb, yb, yb, w_out, mix_g],
                     x, p, layer, *tail, ts)


def _channel_dft_kernel(cs_ref, wf_ref, o_ref):
    for g in range(GROUPS):
        for l in range(2):
            o_ref[g, :, l * LANES:(l + 1) * LANES] = jnp.dot(
                cs_ref[l], wf_ref[g], precision=lax.Precision.HIGHEST, preferred_element_type=F32).astype(BF16)


def _channel_dft_weights(w_f, s):
    n = lax.broadcasted_iota(jnp.int32, (LANES, LANES), 0) * lax.broadcasted_iota(jnp.int32, (LANES, LANES), 1)
    ang = (n % LANES).astype(F32) * (2.0 * math.pi / LANES)
    scale = 1.0 / math.sqrt(float(s) * LANES)
    cs = jnp.stack([jnp.cos(ang) * scale, -jnp.sin(ang) * scale])
    return pl.pallas_call(
        _channel_dft_kernel,
        out_shape=jax.ShapeDtypeStruct((GROUPS, LANES, 2 * LANES), BF16),
        name="channel_dft_weights",
    )(cs, w_f)


def _seq_dft_table():
    shape = (DFT_N2, DFT_N1, DFT_N1)
    n2 = lax.broadcasted_iota(jnp.int32, shape, 0)
    k1 = lax.broadcasted_iota(jnp.int32, shape, 1)
    n1 = lax.broadcasted_iota(jnp.int32, shape, 2)
    s = DFT_N1 * DFT_N2
    ang = (((n2 + DFT_N2 * n1) * k1) % s).astype(F32) * (-2.0 * math.pi / s)
    return jnp.concatenate([jnp.cos(ang), jnp.sin(ang)], axis=1).astype(BF16)


def _rows(v):
    return v[:, None, :]


def kernel(x, p, mix_pre_g, mix_post_g, ffn_pre_g, ffn_post_g, ev_w_in, ev_w_fourier, ev_v_ln_g, ev_v_ln_b, ev_w_spatial, ev_b_spatial, ev_w_out, od_w_in, od_conv_w, od_conv_b, od_ln_g, od_ln_b, od_sconv_w, od_w_out, ffn_w_up, ffn_conv_w, ffn_conv_b, ffn_w_down, ple_w_p, ple_gate_g, ple_w_g, ple_b_g):
    depth = mix_pre_g.shape[0]
    s = x.shape[1]
    ts = min(512, s)
    tab = _seq_dft_table()
    mix_pre_g, mix_post_g = _rows(mix_pre_g), _rows(mix_post_g)
    ffn_tail = (_rows(ffn_pre_g), ffn_w_up.astype(BF16), ffn_conv_w, _rows(ffn_conv_b), ffn_w_down.astype(BF16),
                _rows(ffn_post_g), _rows(ple_gate_g), ple_w_g.astype(BF16), _rows(ple_b_g), ple_w_p.astype(BF16))
    ev_w_in, ev_w_spatial, ev_w_out = ev_w_in.astype(BF16), ev_w_spatial.astype(BF16), ev_w_out.astype(BF16)
    ev_v_ln_g, ev_v_ln_b = _rows(ev_v_ln_g), _rows(ev_v_ln_b)
    od_w_in, od_w_out = od_w_in.astype(BF16), od_w_out.astype(BF16)
    od_conv_b, od_ln_g, od_ln_b = _rows(od_conv_b), _rows(od_ln_g), _rows(od_ln_b)
    for i in range(depth):
        j = i // 2
        if i % 2 == 0:
            wf = _channel_dft_weights(ev_w_fourier[j], s)
            bs = jnp.broadcast_to(ev_b_spatial[j][:, :, None], (GROUPS, CHUNK, LANES))
            z, yb = _even_in(x, i, j, mix_pre_g, ev_w_in, wf, ev_v_ln_g, ev_v_ln_b, ev_w_spatial, bs, ts)
            ya = _seq_dft(tab, z)
            x = _even_tail(x, ya, yb, j, ev_w_out, mix_post_g, p, i, *ffn_tail, ts=ts)
        else:
            x = _odd_mix(x, i, j, mix_pre_g, od_w_in, od_conv_w, od_conv_b, od_ln_g, od_ln_b, od_sconv_w,
                         od_w_out, mix_post_g, ts)
            x = _ffn_ple(x, p, i, *ffn_tail, ts=ts)
    return x
```
